```python
import math
import jax, jax.numpy as jnp
from jax import lax
import numpy as np

D_MODEL = 1024
BATCH = 16
SEQ = 2048
DEPTH = 1

CTX_LEN = 256
GRID_W = 64
A_HEADS = 4
A_DK = 128
A_DV = 128
A_KW = A_HEADS * A_DK
A_VW = A_HEADS * A_DV
A_CHUNK = 64
B_WIDTH = 512
HYENA_ORDER = 2
HYENA_SHORT = 3
HYENA_EMB = 33
HYENA_BANDS = (HYENA_EMB - 1) // 2
HYENA_FFN = 64
HYENA_FAST_DECAY = 0.3
HYENA_SLOW_DECAY = 1.5
HYENA_TARGET = 1e-2
HYENA_SHIFT = 0.05
HYENA_L1_EPS = 1e-6
D_FF = 2816
N_MOD = 9
RMS_EPS = 1e-6
COL_V = 0
COL_FFW = COL_V + A_VW
COL_FBW = COL_FFW + A_KW
COL_Q = COL_FBW + A_KW
COL_G = COL_Q + A_KW
COL_HY = COL_G + A_VW
COL_MERGE = COL_HY + 3 * B_WIDTH
IN_COLS = COL_MERGE + 2 * D_MODEL

kernel_name = 'hgrn2_hyena_macaron_dit_layer'


def rms_norm(x):
    xf = x.astype(jnp.float32)
    return (xf * lax.rsqrt(jnp.mean(xf * xf, axis=-1, keepdims=True) + RMS_EPS)).astype(x.dtype)


def modulate(x, shift, scale):
    return x * (1.0 + scale) + shift


def swiglu(x, wg, wu, wd):
    return (jax.nn.silu(x @ wg) * (x @ wu)) @ wd


def half_ffn(h, shift, scale, gate, wg, wu, wd):
    return h + 0.5 * gate * swiglu(modulate(rms_norm(h), shift, scale), wg, wu, wd)


def grid_pos_embed(n_tokens):
    rows = n_tokens // GRID_W
    quarter = D_MODEL // 4
    omega = 1.0 / (10000.0 ** (jnp.arange(quarter, dtype=jnp.float32) / quarter))
    ar = jnp.arange(rows, dtype=jnp.float32)[:, None] * omega
    ac = jnp.arange(GRID_W, dtype=jnp.float32)[:, None] * omega
    er = jnp.concatenate([jnp.sin(ar), jnp.cos(ar)], axis=-1)
    ec = jnp.concatenate([jnp.sin(ac), jnp.cos(ac)], axis=-1)
    emb = jnp.concatenate([jnp.broadcast_to(er[:, None, :], (rows, GRID_W, D_MODEL // 2)),
                           jnp.broadcast_to(ec[None, :, :], (rows, GRID_W, D_MODEL // 2))], axis=-1)
    return emb.reshape(rows * GRID_W, D_MODEL)


def to_heads(a):
    b, l, w = a.shape
    return a.reshape(b, l, A_HEADS, w // A_HEADS).transpose(0, 2, 1, 3)


def hgrn2_forget(z, lb):
    f = lb + (1.0 - lb) * jax.nn.sigmoid(z.astype(jnp.float32))
    return to_heads(jnp.log(f)), to_heads(1.0 - f)


def hgrn2_chunk_scan(q, k, v, logf, s0):
    b, h, l, dk = q.shape
    n_chunks = l // A_CHUNK

    def to_chunks(a):
        return a.reshape(b, h, n_chunks, A_CHUNK, a.shape[-1]).transpose(2, 0, 1, 3, 4)

    within_scan = jnp.tril(jnp.ones((A_CHUNK, A_CHUNK), dtype=bool))[:, :, None]

    def step(s, inp):
        qb, kb, vb, gb = inp
        cum = jnp.cumsum(gb, axis=2)
        diff = cum[:, :, :, None, :] - cum[:, :, None, :, :]
        decay = jnp.exp(jnp.where(within_scan, diff, -jnp.inf))
        scores = jnp.einsum('bhtsd,bhsd->bhts', qb[:, :, :, None, :] * decay, kb)
        o = scores @ vb + (qb * jnp.exp(cum)) @ s
        last = cum[:, :, -1:, :]
        s_new = jnp.exp(last[:, :, 0, :])[..., None] * s + jnp.einsum('bhsd,bhse->bhde', kb * jnp.exp(last - cum), vb)
        return s_new, o

    s_final, oc = lax.scan(step, s0, (to_chunks(q), to_chunks(k), to_chunks(v), to_chunks(logf)))
    o = oc.transpose(1, 2, 0, 3, 4).reshape(b, h, l, v.shape[-1])
    return o, s_final


def hgrn2_final_state(k, v, logf):
    cum = jnp.cumsum(logf, axis=2)
    return jnp.einsum('bhld,bhle->bhde', k * jnp.exp(cum[:, :, -1:, :] - cum), v)


def flip_seq(a):
    return jnp.flip(a, axis=2)


def hgrn2_bidir(q, v, logf_fw, k_fw, logf_bw, k_bw, s0_fw, s0_bw):
    o_fw, s_fw = hgrn2_chunk_scan(q, k_fw, v, logf_fw, s0_fw)
    o_bw, s_bw = hgrn2_chunk_scan(flip_seq(q), flip_seq(k_bw), flip_seq(v), flip_seq(logf_bw), s0_bw)
    return o_fw + flip_seq(o_bw), s_fw, s_bw


def context_states(nc, w_in, lb):
    p = nc @ w_in[:, :COL_Q]
    v = to_heads(p[..., COL_V:COL_FFW]).astype(jnp.float32)
    logf_fw, k_fw = hgrn2_forget(p[..., COL_FFW:COL_FBW], lb[0])
    logf_bw, k_bw = hgrn2_forget(p[..., COL_FBW:COL_Q], lb[1])
    s_fw = hgrn2_final_state(k_fw, v, logf_fw)
    s_bw = hgrn2_final_state(flip_seq(k_bw), flip_seq(v), flip_seq(logf_bw))
    return s_fw, s_bw


def short_conv(u, w, bias):
    l = u.shape[1]
    pad = HYENA_SHORT // 2
    up = jnp.pad(u, ((0, 0), (pad, pad), (0, 0)))
    out = bias
    for j in range(HYENA_SHORT):
        out = out + up[:, j:j + l] * w[j]
    return out


def hyena_pos_features(l):
    p = jnp.arange(l, dtype=jnp.float32)
    t = p / (l - 1)
    w = 2.0 * math.pi * p / l
    f = jnp.linspace(1e-4, HYENA_BANDS - 1, HYENA_BANDS, dtype=jnp.float32)
    ang = w[:, None] * f[None, :]
    z = jnp.concatenate([t[:, None], jnp.cos(ang), -jnp.sin(ang)], axis=-1)
    return t, z


def hyena_kernels(l, w1, b1, fr1, w2, b2, fr2, w3):
    t, z = hyena_pos_features(l)
    h = jnp.sin(fr1 * (z @ w1 + b1))
    h = jnp.sin(fr2 * (h @ w2 + b2))
    h = (h @ w3).astype(jnp.float32).reshape(l, HYENA_ORDER, 2, B_WIDTH)
    max_decay = math.log(HYENA_TARGET) / HYENA_FAST_DECAY
    min_decay = math.log(HYENA_TARGET) / HYENA_SLOW_DECAY
    deltas = jnp.abs(jnp.linspace(min_decay, max_decay, B_WIDTH, dtype=jnp.float32))
    window = jnp.exp(-t[:, None] * deltas[None, :]) + HYENA_SHIFT
    h = h * window[:, None, None, :]
    kern = jnp.concatenate([h[:, :, 0], jnp.flip(h[:, :, 1], axis=0)], axis=0)
    return kern / (jnp.sum(jnp.abs(kern), axis=0, keepdims=True) + HYENA_L1_EPS)


def fft_conv(u, kern):
    l = u.shape[1]
    uf = jnp.fft.rfft(u.astype(jnp.float32), n=2 * l, axis=1)
    kf = jnp.fft.rfft(kern, n=2 * l, axis=0)
    return jnp.fft.irfft(uf * kf[None], n=2 * l, axis=1)[:, :l]


def hyena(u3, kern, hy_bias):
    v, x1, x2 = jnp.split(u3, 3, axis=-1)
    z = v
    for o, gate in enumerate((x1, x2)):
        z = gate * (fft_conv(z, kern[:, o]).astype(z.dtype) + hy_bias[o] * z)
    return z


def token_mixers(n, s0_fw, s0_bw, kern, w_in, lb, a_norm_w, conv_w, conv_b, hy_bias, w_pa, w_pb, w_out):
    b, l, _ = n.shape
    p = n @ w_in
    v = to_heads(p[..., COL_V:COL_FFW]).astype(jnp.float32)
    logf_fw, k_fw = hgrn2_forget(p[..., COL_FFW:COL_FBW], lb[0])
    logf_bw, k_bw = hgrn2_forget(p[..., COL_FBW:COL_Q], lb[1])
    q = to_heads(jax.nn.silu(p[..., COL_Q:COL_G])).astype(jnp.float32)
    o, s_fw, s_bw = hgrn2_bidir(q, v, logf_fw, k_fw, logf_bw, k_bw, s0_fw, s0_bw)
    o = rms_norm(o.transpose(0, 2, 1, 3)) * a_norm_w
    o_a = o.reshape(b, l, A_VW).astype(n.dtype) * jax.nn.silu(p[..., COL_G:COL_HY])
    u3 = short_conv(p[..., COL_HY:COL_MERGE], conv_w, conv_b)
    o_b = hyena(u3, kern, hy_bias)
    g_a, g_b = jnp.split(p[..., COL_MERGE:], 2, axis=-1)
    y = jax.nn.sigmoid(g_a) * (o_a @ w_pa) + jax.nn.sigmoid(g_b) * (o_b @ w_pb)
    return y @ w_out, s_fw, s_bw


def setup_inputs(seed: int = 0) -> dict:
    key = jax.random.key(seed)
    ks = jax.random.split(key, 32)
    nrm = jax.random.normal
    f32 = jnp.float32
    d = D_MODEL
    return {
        'x': nrm(ks[0], (BATCH, SEQ, d), f32),
        'c': nrm(ks[1], (BATCH, d), f32),
        'ctx': nrm(ks[2], (BATCH, CTX_LEN, d), f32),
        'c_ctx': nrm(ks[3], (d,), f32),
        'mod_w': nrm(ks[4], (DEPTH, d, N_MOD * d), f32) * d ** -0.5,
        'mod_b': nrm(ks[5], (DEPTH, N_MOD * d), f32) * 0.01,
        'ffn_w_gate': nrm(ks[6], (DEPTH, 2, d, D_FF), f32) * d ** -0.5,
        'ffn_w_up': nrm(ks[7], (DEPTH, 2, d, D_FF), f32) * d ** -0.5,
        'ffn_w_down': nrm(ks[8], (DEPTH, 2, D_FF, d), f32) * D_FF ** -0.5,
        'w_in': nrm(ks[9], (DEPTH, d, IN_COLS), f32) * d ** -0.5,
        'hgrn_lb_logits': nrm(ks[10], (DEPTH + 1, 2, A_KW), f32) * 0.1,
        'hgrn_norm_w': 1.0 + 0.1 * nrm(ks[11], (DEPTH, A_DV), f32),
        'hyena_conv_w': nrm(ks[12], (DEPTH, HYENA_SHORT, 3 * B_WIDTH), f32) * HYENA_SHORT ** -0.5,
        'hyena_conv_b': nrm(ks[13], (DEPTH, 3 * B_WIDTH), f32) * 0.01,
        'hyena_w1': nrm(ks[14], (DEPTH, HYENA_EMB, HYENA_FFN), f32) * HYENA_EMB ** -0.5,
        'hyena_b1': nrm(ks[15], (DEPTH, HYENA_FFN), f32) * 0.1,
        'hyena_freq1': 1.0 + 0.1 * nrm(ks[16], (DEPTH, HYENA_FFN), f32),
        'hyena_w2': nrm(ks[17], (DEPTH, HYENA_FFN, HYENA_FFN), f32) * HYENA_FFN ** -0.5,
        'hyena_b2': nrm(ks[18], (DEPTH, HYENA_FFN), f32) * 0.1,
        'hyena_freq2': 1.0 + 0.1 * nrm(ks[19], (DEPTH, HYENA_FFN), f32),
        'hyena_w3': nrm(ks[20], (DEPTH, HYENA_FFN, HYENA_ORDER * 2 * B_WIDTH), f32) * HYENA_FFN ** -0.5,
        'hyena_bias': nrm(ks[21], (DEPTH, HYENA_ORDER, B_WIDTH), f32),
        'w_proj_a': nrm(ks[22], (DEPTH, A_VW, d), f32) * A_VW ** -0.5,
        'w_proj_b': nrm(ks[23], (DEPTH, B_WIDTH, d), f32) * B_WIDTH ** -0.5,
        'w_out': nrm(ks[24], (DEPTH, d, d), f32) * d ** -0.5,
        'final_norm_w': 1.0 + 0.1 * nrm(ks[25], (d,), f32),
    }


def reference(x, c, ctx, c_ctx, mod_w, mod_b, ffn_w_gate, ffn_w_up, ffn_w_down, w_in, hgrn_lb_logits,
              hgrn_norm_w, hyena_conv_w, hyena_conv_b, hyena_w1, hyena_b1, hyena_freq1, hyena_w2, hyena_b2,
              hyena_freq2, hyena_w3, hyena_bias, w_proj_a, w_proj_b, w_out, final_norm_w):
    n_lat = x.shape[1]
    n_ctx = ctx.shape[1]
    h = x + grid_pos_embed(n_lat).astype(x.dtype)[None]
    hc = ctx
    lb_all = jnp.cumsum(jax.nn.softmax(hgrn_lb_logits.astype(jnp.float32), axis=0), axis=0)
    for l in range(DEPTH):
        last = l == DEPTH - 1
        m = jnp.split((jax.nn.silu(c) @ mod_w[l] + mod_b[l])[:, None, :], N_MOD, axis=-1)
        mc = jnp.split((jax.nn.silu(c_ctx) @ mod_w[l] + mod_b[l])[None, None, :], N_MOD, axis=-1)
        h = half_ffn(h, m[0], m[1], m[2], ffn_w_gate[l, 0], ffn_w_up[l, 0], ffn_w_down[l, 0])
        hc = half_ffn(hc, mc[0], mc[1], mc[2], ffn_w_gate[l, 0], ffn_w_up[l, 0], ffn_w_down[l, 0])
        n = modulate(rms_norm(h), m[3], m[4])
        nc = modulate(rms_norm(hc), mc[3], mc[4])
        mix_w = (w_in[l], lb_all[l], hgrn_norm_w[l], hyena_conv_w[l], hyena_conv_b[l], hyena_bias[l],
                 w_proj_a[l], w_proj_b[l], w_out[l])
        filt_w = (hyena_w1[l], hyena_b1[l], hyena_freq1[l], hyena_w2[l], hyena_b2[l], hyena_freq2[l], hyena_w3[l])
        if last:
            s_fw, s_bw = context_states(nc, w_in[l], lb_all[l])
        else:
            zeros = jnp.zeros((hc.shape[0], A_HEADS, A_DK, A_DV), jnp.float32)
            out_c, s_fw, s_bw = token_mixers(nc, zeros, zeros, hyena_kernels(n_ctx, *filt_w), *mix_w)
            hc = hc + mc[5] * out_c
            hc = half_ffn(hc, mc[6], mc[7], mc[8], ffn_w_gate[l, 1], ffn_w_up[l, 1], ffn_w_down[l, 1])
        out, _, _ = token_mixers(n, s_fw, s_bw, hyena_kernels(n_lat, *filt_w), *mix_w)
        h = h + m[5] * out
        h = half_ffn(h, m[6], m[7], m[8], ffn_w_gate[l, 1], ffn_w_up[l, 1], ffn_w_down[l, 1])
    return rms_norm(h) * final_norm_w
```

```python
import functools
import math

import jax
import jax.numpy as jnp
from jax import lax
from jax.experimental import pallas as pl
from jax.experimental.pallas import tpu as pltpu

F32 = jnp.float32
BF16 = jnp.bfloat16
HIGHEST = lax.Precision.HIGHEST

D_MODEL = 1024
GRID_W = 64
HEADS = 4
HEAD_DIM = 128
KW = HEADS * HEAD_DIM
HY_W = 512
D_FF = 2816
N_MOD = 9
RMS_EPS = 1e-6
HYENA_EMB = 33
HYENA_BANDS = (HYENA_EMB - 1) // 2
HYENA_FFN = 64
HYENA_FAST_DECAY = 0.3
HYENA_SLOW_DECAY = 1.5
HYENA_TARGET = 1e-2
HYENA_SHIFT = 0.05
HYENA_L1_EPS = 1e-6

COL_V = 0
COL_FFW = COL_V + KW
COL_FBW = COL_FFW + KW
COL_Q = COL_FBW + KW
COL_G = COL_Q + KW
COL_HY = COL_G + KW
COL_MERGE = COL_HY + 3 * HY_W
IN_COLS = COL_MERGE + 2 * D_MODEL

V7X_VMEM_BYTES = 64 * 1024 * 1024
VMEM_LIMIT = V7X_VMEM_BYTES - 8 * 1024 * 1024

TOKEN_TILE = 512
FF_CHUNK = 256
SCAN_CHUNK = 64
FREQ_TILE = 256
FILT_TILE = 256
FEAT_PAD = 128


def _const_spec(shape):
    nd = len(shape)
    return pl.BlockSpec(shape, lambda *_: (0,) * nd, pipeline_mode=pl.Buffered(1))


def _params(n_grid):
    return pltpu.CompilerParams(dimension_semantics=("arbitrary",) * n_grid, vmem_limit_bytes=VMEM_LIMIT)


def _rms(x):
    return x * lax.rsqrt(jnp.mean(x * x, axis=-1, keepdims=True) + RMS_EPS)


def _norm_mod(h, shift, scale):
    return _rms(h) * (1.0 + scale) + shift


def _dot(a, b):
    return jnp.dot(a, b, preferred_element_type=F32)


def _mod_kernel(c_ref, w_ref, b_ref, o_ref):
    c = c_ref[...]
    a = c * jax.nn.sigmoid(c)
    o_ref[...] = jnp.dot(a, w_ref[...], precision=HIGHEST, preferred_element_type=F32) + b_ref[...]


def _modulation(c_all, mod_w, mod_b):
    rows = c_all.shape[0]
    tn = D_MODEL
    return pl.pallas_call(
        _mod_kernel,
        out_shape=jax.ShapeDtypeStruct((rows, N_MOD * D_MODEL), F32),
        grid=(N_MOD,),
        in_specs=[
            pl.BlockSpec((rows, D_MODEL), lambda j: (0, 0)),
            pl.BlockSpec((D_MODEL, tn), lambda j: (0, j)),
            pl.BlockSpec((1, tn), lambda j: (0, j)),
        ],
        out_specs=pl.BlockSpec((rows, tn), lambda j: (0, j)),
        compiler_params=_params(1),
        name="mod",
    )(c_all, mod_w, mod_b)


def _ffn_kernel(*refs, mod_base, add_pos, final_norm):
    refs = list(refs)
    h_ref = refs.pop(0)
    pos_ref = refs.pop(0) if add_pos else None
    m_ref, wg_ref, wu_ref, wd_ref = refs[:4]
    refs = refs[4:]
    fnw_ref = refs.pop(0) if final_norm else None
    o_ref = refs.pop(0)

    h = h_ref[0]
    if add_pos:
        h = h + pos_ref[...]
    shift = m_ref[0, mod_base:mod_base + 1, :]
    scale = m_ref[0, mod_base + 1:mod_base + 2, :]
    gate = m_ref[0, mod_base + 2:mod_base + 3, :]
    nb = _norm_mod(h, shift, scale).astype(BF16)
    acc = jnp.zeros(h.shape, F32)
    for c in range(D_FF // FF_CHUNK):
        sl = slice(c * FF_CHUNK, (c + 1) * FF_CHUNK)
        g = _dot(nb, wg_ref[:, sl])
        u = _dot(nb, wu_ref[:, sl])
        a = (g * jax.nn.sigmoid(g) * u).astype(BF16)
        acc = acc + _dot(a, wd_ref[sl, :])
    out = h + 0.5 * gate * acc
    if final_norm:
        out = _rms(out) * fnw_ref[...]
    o_ref[0] = out


def _half_ffn(h, m3, wg, wu, wd, *, mod_base, mod_row=None, pos=None, final_norm_w=None):
    b, l, d = h.shape
    tm = min(TOKEN_TILE, l)
    nt = l // tm
    row_map = (lambda j, i: (i, 0, 0)) if mod_row is None else (lambda j, i: (mod_row, 0, 0))
    in_specs = [pl.BlockSpec((1, tm, d), lambda j, i: (i, j, 0))]
    args = [h]
    if pos is not None:
        in_specs.append(pl.BlockSpec((tm, d), lambda j, i: (j, 0)))
        args.append(pos)
    in_specs += [
        pl.BlockSpec((1, N_MOD, d), row_map),
        _const_spec(wg.shape), _const_spec(wu.shape), _const_spec(wd.shape),
    ]
    args += [m3, wg, wu, wd]
    if final_norm_w is not None:
        in_specs.append(_const_spec(final_norm_w.shape))
        args.append(final_norm_w)
    kern = functools.partial(_ffn_kernel, mod_base=mod_base, add_pos=pos is not None,
                             final_norm=final_norm_w is not None)
    return pl.pallas_call(
        kern,
        out_shape=jax.ShapeDtypeStruct((b, l, d), F32),
        grid=(nt, b),
        in_specs=in_specs,
        out_specs=pl.BlockSpec((1, tm, d), lambda j, i: (i, j, 0)),
        compiler_params=_params(2),
        name="ffn",
    )(*args)


def _log_forget(z, lb):
    return jnp.log(lb + (1.0 - lb) * jax.nn.sigmoid(z))


def _proj_kernel(h_ref, m_ref, w_ref, lb_ref, *out_refs, full):
    h = h_ref[0]
    nb = _norm_mod(h, m_ref[0, 3:4, :], m_ref[0, 4:5, :]).astype(BF16)

    def proj(c0):
        return _dot(nb, w_ref[:, c0:c0 + KW])

    v_ref, lff_ref, lfb_ref = out_refs[:3]
    v_ref[0] = proj(COL_V).astype(BF16)
    lff_ref[0] = _log_forget(proj(COL_FFW), lb_ref[0:1, :])
    lfb_ref[0] = _log_forget(proj(COL_FBW), lb_ref[1:2, :])
    if full:
        q_ref, g_ref, hy_ref, mg_ref = out_refs[3:]
        zq = proj(COL_Q)
        q_ref[0] = (zq * jax.nn.sigmoid(zq)).astype(BF16)
        zg = proj(COL_G)
        g_ref[0] = (zg * jax.nn.sigmoid(zg)).astype(BF16)
        for k in range(3 * HY_W // KW):
            hy_ref[0, :, k * KW:(k + 1) * KW] = proj(COL_HY + k * KW).astype(BF16)
        for k in range(2 * D_MODEL // KW):
            mg_ref[0, :, k * KW:(k + 1) * KW] = jax.nn.sigmoid(proj(COL_MERGE + k * KW)).astype(BF16)


def _input_proj(h, m3, w_in, lb, *, mod_row=None):
    b, l, d = h.shape
    full = mod_row is None
    tm = min(TOKEN_TILE, l)
    nt = l // tm
    row_map = (lambda j, i: (i, 0, 0)) if full else (lambda j, i: (mod_row, 0, 0))
    ncols = IN_COLS if full else COL_Q
    tok = lambda w: pl.BlockSpec((1, tm, w), lambda j, i: (i, j, 0))
    shapes = [(KW, BF16), (KW, F32), (KW, F32)]
    if full:
        shapes += [(KW, BF16), (KW, BF16), (3 * HY_W, BF16), (2 * D_MODEL, BF16)]
    return pl.pallas_call(
        functools.partial(_proj_kernel, full=full),
        out_shape=[jax.ShapeDtypeStruct((b, l, w), dt) for w, dt in shapes],
        grid=(nt, b),
        in_specs=[
            tok(d),
            pl.BlockSpec((1, N_MOD, d), row_map),
            pl.BlockSpec((d, ncols), lambda j, i: (0, 0), pipeline_mode=pl.Buffered(1)),
            _const_spec(lb.shape),
        ],
        out_specs=[tok(w) for w, _ in shapes],
        compiler_params=_params(2),
        name="proj" if full else "proj_ctx",
    )(h, m3, w_in, lb)


def _split3(x):
    hi = x.astype(BF16)
    r = x - hi.astype(F32)
    mid = r.astype(BF16)
    lo = (r - mid.astype(F32)).astype(BF16)
    return hi, mid, lo


def _scan_kernel(q_ref, v_ref, lff_ref, lfb_ref, g_ref, vc_ref, lffc_ref, lfbc_ref, nw_ref,
                 o_ref, ofw_ref, s_ref):
    c = SCAN_CHUNK
    n_lat = q_ref.shape[1] // c
    n_ctx = vc_ref.shape[1] // c
    row = lax.broadcasted_iota(jnp.int32, (c, c), 0)
    col = lax.broadcasted_iota(jnp.int32, (c, c), 1)
    lower = row >= col
    upper = row <= col
    tri_l = jnp.where(lower, 1.0, 0.0).astype(BF16)
    tri_u = jnp.where(upper, 1.0, 0.0).astype(BF16)

    def chunk(direction, lf, v, q):
        if direction == 0:
            t_rows, t_cols, mask, ref_i, tot_i = tri_l, tri_u, lower, c // 2 - 1, c - 1
        else:
            t_rows, t_cols, mask, ref_i, tot_i = tri_u, tri_l, upper, c // 2, 0
        lft = lf.T
        hi, mid, lo = _split3(lft)
        bt = _dot(hi, t_cols) + _dot(mid, t_cols) + _dot(lo, t_cols)
        kt = 1.0 - jnp.exp(lft)
        tot_c = bt[:, tot_i:tot_i + 1]
        kdt = (kt * jnp.exp(tot_c - bt)).astype(BF16)
        dec_c = jnp.exp(tot_c)
        if q is not None:
            hi, mid, lo = _split3(lf)
            b = _dot(t_rows, hi) + _dot(t_rows, mid) + _dot(t_rows, lo)
            qf = q.astype(F32)
            qm = (qf * jnp.exp(b - b[ref_i:ref_i + 1, :])).astype(BF16)
            qd = (qf * jnp.exp(b)).astype(BF16)
            kmt = (kt * jnp.exp(bt[:, ref_i:ref_i + 1] - bt)).astype(BF16)
        outs = []
        for hd in range(HEADS):
            hs = slice(hd * HEAD_DIM, (hd + 1) * HEAD_DIM)
            s = s_ref[direction, hd]
            vh = v[:, hs]
            if q is not None:
                a = jnp.where(mask, _dot(qm[:, hs], kmt[hs, :]), 0.0).astype(BF16)
                outs.append(_dot(a, vh) + _dot(qd[:, hs], s.astype(BF16)))
            s_ref[direction, hd] = dec_c[hs, :] * s + _dot(kdt[hs, :], vh)
        return None if q is None else jnp.concatenate(outs, axis=-1)

    s_ref[...] = jnp.zeros(s_ref.shape, F32)
    for j in range(n_ctx):
        rows = slice(j * c, (j + 1) * c)
        chunk(0, lffc_ref[0, rows, :], vc_ref[0, rows, :], None)
    for j in reversed(range(n_ctx)):
        rows = slice(j * c, (j + 1) * c)
        chunk(1, lfbc_ref[0, rows, :], vc_ref[0, rows, :], None)

    def fw_body(j, carry):
        rows = pl.ds(pl.multiple_of(j * c, c), c)
        ofw_ref[rows, :] = chunk(0, lff_ref[0, rows, :], v_ref[0, rows, :], q_ref[0, rows, :])
        return carry

    lax.fori_loop(0, n_lat, fw_body, 0)

    nw = nw_ref[...]

    def bw_body(i, carry):
        rows = pl.ds(pl.multiple_of((n_lat - 1 - i) * c, c), c)
        o = ofw_ref[rows, :] + chunk(1, lfb_ref[0, rows, :], v_ref[0, rows, :], q_ref[0, rows, :])
        normed = [_rms(o[:, hd * HEAD_DIM:(hd + 1) * HEAD_DIM]) * nw for hd in range(HEADS)]
        o_ref[0, rows, :] = (jnp.concatenate(normed, axis=-1) * g_ref[0, rows, :].astype(F32)).astype(BF16)
        return carry

    lax.fori_loop(0, n_lat, bw_body, 0)


def _hgrn2(q, v, lff, lfb, g, vc, lffc, lfbc, norm_w):
    b, l, _ = q.shape
    lc = vc.shape[1]
    seq = lambda n: pl.BlockSpec((1, n, KW), lambda i: (i, 0, 0))
    return pl.pallas_call(
        _scan_kernel,
        out_shape=jax.ShapeDtypeStruct((b, l, KW), BF16),
        grid=(b,),
        in_specs=[seq(l), seq(l), seq(l), seq(l), seq(l), seq(lc), seq(lc), seq(lc),
                  _const_spec(norm_w.shape)],
        out_specs=seq(l),
        scratch_shapes=[pltpu.VMEM((l, KW), F32), pltpu.VMEM((2, HEADS, HEAD_DIM, HEAD_DIM), F32)],
        compiler_params=_params(1),
        name="scan",
    )(q, v, lff, lfb, g, vc, lffc, lfbc, norm_w)


def _filter_kernel(z_ref, w1_ref, b1_ref, f1_ref, w2_ref, b2_ref, f2_ref, w3_ref, win_ref, h_ref, mass_ref):
    hp = dict(precision=HIGHEST, preferred_element_type=F32)
    h = jnp.sin(f1_ref[...] * (jnp.dot(z_ref[...], w1_ref[...], **hp) + b1_ref[...]))
    h = jnp.sin(f2_ref[...] * (jnp.dot(h, w2_ref[...], **hp) + b2_ref[...]))
    h = jnp.dot(h, w3_ref[...], **hp)
    win = win_ref[...]
    n_groups = h.shape[1] // HY_W
    h = jnp.concatenate([h[:, k * HY_W:(k + 1) * HY_W] * win for k in range(n_groups)], axis=-1)
    h_ref[...] = h

    @pl.when(pl.program_id(0) == 0)
    def _():
        mass_ref[...] = jnp.zeros(mass_ref.shape, F32)

    mass_ref[...] += jnp.sum(jnp.abs(h), axis=0, keepdims=True)


def _hyena_filters(l, w1, b1, fr1, w2, b2, fr2, w3):
    p = jnp.arange(l, dtype=F32)
    t = p / (l - 1)
    w = 2.0 * math.pi * p / l
    f = jnp.linspace(1e-4, HYENA_BANDS - 1, HYENA_BANDS, dtype=F32)
    ang = w[:, None] * f[None, :]
    z = jnp.concatenate([t[:, None], jnp.cos(ang), -jnp.sin(ang)], axis=-1)
    max_decay = math.log(HYENA_TARGET) / HYENA_FAST_DECAY
    min_decay = math.log(HYENA_TARGET) / HYENA_SLOW_DECAY
    deltas = jnp.abs(jnp.linspace(min_decay, max_decay, HY_W, dtype=F32))
    window = jnp.exp(-t[:, None] * deltas[None, :]) + HYENA_SHIFT

    pad_c = lambda a, n: jnp.pad(a, ((0, 0), (0, n - a.shape[1])))
    pad_r = lambda a, n: jnp.pad(a, ((0, n - a.shape[0]), (0, 0)))
    z = pad_c(z, FEAT_PAD)
    w1p = pad_c(pad_r(w1, FEAT_PAD), FEAT_PAD)
    w2p = pad_c(pad_r(w2, FEAT_PAD), FEAT_PAD)
    w3p = pad_r(w3, FEAT_PAD)
    vec = lambda a: pad_c(a[None, :], FEAT_PAD)
    n_out = w3.shape[1]
    tl = FILT_TILE
    return pl.pallas_call(
        _filter_kernel,
        out_shape=[jax.ShapeDtypeStruct((l, n_out), F32), jax.ShapeDtypeStruct((1, n_out), F32)],
        grid=(l // tl,),
        in_specs=[
            pl.BlockSpec((tl, FEAT_PAD), lambda i: (i, 0)),
            _const_spec(w1p.shape), _const_spec((1, FEAT_PAD)), _const_spec((1, FEAT_PAD)),
            _const_spec(w2p.shape), _const_spec((1, FEAT_PAD)), _const_spec((1, FEAT_PAD)),
            _const_spec(w3p.shape),
            pl.BlockSpec((tl, HY_W), lambda i: (i, 0)),
        ],
        out_specs=[pl.BlockSpec((tl, n_out), lambda i: (i, 0)), pl.BlockSpec((1, n_out), lambda i: (0, 0))],
        compiler_params=_params(1),
        name="filt",
    )(z, w1p, vec(b1), vec(fr1), w2p, vec(b2), vec(fr2), w3p, window)


def _odd_dft_matrices(l):
    n_ang = 4 * l
    lo_n = 64
    f2 = 2 * jnp.arange(l, dtype=jnp.int32) + 1
    theta = 2.0 * math.pi / n_ang

    def cis(idx):
        a = (idx % n_ang).astype(F32) * theta
        return jnp.cos(a), jnp.sin(a)

    hr, hi = cis(f2[:, None] * (lo_n * jnp.arange(l // lo_n, dtype=jnp.int32))[None, :])
    lr, li = cis(f2[:, None] * jnp.arange(lo_n, dtype=jnp.int32)[None, :])
    cos = (hr[:, :, None] * lr[:, None, :] - hi[:, :, None] * li[:, None, :]).reshape(l, l)
    sin = (hr[:, :, None] * li[:, None, :] + hi[:, :, None] * lr[:, None, :]).reshape(l, l)
    nf = l // FREQ_TILE
    fwd = jnp.stack([cos.reshape(nf, FREQ_TILE, l), sin.reshape(nf, FREQ_TILE, l)], axis=1)
    fwd = fwd.reshape(2 * l, l).astype(BF16)
    return fwd, fwd.T


def _kdft_kernel(fwd_ref, kk_ref, nrm_ref, o_ref):
    tf = FREQ_TILE
    half = kk_ref.shape[1] // 2
    r = _dot(fwd_ref[...], kk_ref[...])
    a_c, a_s = r[:tf, :half], r[tf:, :half]
    b_c, b_s = r[:tf, half:], r[tf:, half:]
    f = pl.program_id(0) * tf + lax.broadcasted_iota(jnp.int32, (tf, 1), 0)
    sgn = jnp.where(f % 2 == 0, 1.0, -1.0)
    scale = (1.0 / kk_ref.shape[0]) / nrm_ref[...]
    o_ref[0, 0] = (a_c - sgn * b_s) * scale
    o_ref[0, 1] = (-a_s - sgn * b_c) * scale


def _filter_spectrum(fwd, kk, nrm):
    l = kk.shape[0]
    tf = FREQ_TILE
    half = kk.shape[1] // 2
    return pl.pallas_call(
        _kdft_kernel,
        out_shape=jax.ShapeDtypeStruct((l // tf, 2, tf, half), F32),
        grid=(l // tf,),
        in_specs=[pl.BlockSpec((2 * tf, l), lambda i: (i, 0)), _const_spec(kk.shape), _const_spec(nrm.shape)],
        out_specs=pl.BlockSpec((1, 2, tf, half), lambda i: (i, 0, 0, 0)),
        compiler_params=_params(1),
        name="kdft",
    )(fwd, kk, nrm)


def _short_conv(x_ref, cw_ref, cb_ref, k):
    x = x_ref[0].astype(F32)
    l = x.shape[0]
    cs = slice(k * HY_W, (k + 1) * HY_W)
    t = lax.broadcasted_iota(jnp.int32, (l, 1), 0)
    prev = jnp.where(t == 0, 0.0, pltpu.roll(x, 1, 0))
    nxt = jnp.where(t == l - 1, 0.0, pltpu.roll(x, l - 1, 0))
    return cb_ref[0:1, cs] + prev * cw_ref[0:1, cs] + x * cw_ref[1:2, cs] + nxt * cw_ref[2:3, cs]


def _hyena_kernel(hv_ref, hx1_ref, hx2_ref, cw_ref, cb_ref, hb_ref, fwd_ref, inv_ref, kt_ref,
                  o_ref, u_ref, acc_ref):
    order = pl.program_id(1)
    i = pl.program_id(2)
    last = pl.num_programs(2) - 1
    tf = FREQ_TILE

    @pl.when((order == 0) & (i == 0))
    def _():
        v = _short_conv(hv_ref, cw_ref, cb_ref, 0)
        u_ref[...] = v.astype(BF16)
        acc_ref[...] = hb_ref[0:1, :] * v

    u = _dot(fwd_ref[...], u_ref[...])
    uc, us = u[:tf], u[tf:]
    kr, ki = kt_ref[0, 0], kt_ref[0, 1]
    y = jnp.concatenate([kr * uc + ki * us, kr * us - ki * uc], axis=0).astype(BF16)
    acc_ref[...] += _dot(inv_ref[...], y)

    @pl.when((order == 0) & (i == last))
    def _():
        z1 = _short_conv(hx1_ref, cw_ref, cb_ref, 1) * acc_ref[...]
        u_ref[...] = z1.astype(BF16)
        acc_ref[...] = hb_ref[1:2, :] * z1

    @pl.when((order == 1) & (i == last))
    def _():
        o_ref[0] = (_short_conv(hx2_ref, cw_ref, cb_ref, 2) * acc_ref[...]).astype(BF16)


def _hyena(hy, conv_w, conv_b, hy_bias, fwd, inv, ktab):
    b, l, _ = hy.shape
    tf = FREQ_TILE
    nf = l // tf
    col = lambda k: pl.BlockSpec((1, l, HY_W), lambda bi, o, i: (bi, 0, k))
    return pl.pallas_call(
        _hyena_kernel,
        out_shape=jax.ShapeDtypeStruct((b, l, HY_W), BF16),
        grid=(b, 2, nf),
        in_specs=[
            col(0), col(1), col(2),
            _const_spec(conv_w.shape), _const_spec(conv_b.shape), _const_spec(hy_bias.shape),
            pl.BlockSpec((2 * tf, l), lambda bi, o, i: (i, 0)),
            pl.BlockSpec((l, 2 * tf), lambda bi, o, i: (0, i)),
            pl.BlockSpec((1, 2, tf, HY_W), lambda bi, o, i: (i, 0, 0, o)),
        ],
        out_specs=pl.BlockSpec((1, l, HY_W), lambda bi, o, i: (bi, 0, 0)),
        scratch_shapes=[pltpu.VMEM((l, HY_W), BF16), pltpu.VMEM((l, HY_W), F32)],
        compiler_params=_params(3),
        name="hyena",
    )(hy, hy, hy, conv_w, conv_b, hy_bias, fwd, inv, ktab)


def _merge_kernel(h_ref, m_ref, oa_ref, ob_ref, sg_ref, wpa_ref, wpb_ref, wo_ref, o_ref):
    d = h_ref.shape[2]
    ya = _dot(oa_ref[0], wpa_ref[...])
    yb = _dot(ob_ref[0], wpb_ref[...])
    y = sg_ref[0, :, :d].astype(F32) * ya + sg_ref[0, :, d:].astype(F32) * yb
    out = _dot(y.astype(BF16), wo_ref[...])
    o_ref[0] = h_ref[0] + m_ref[0, 5:6, :] * out


def _merge(h, m3, o_a, o_b, sg, w_pa, w_pb, w_out):
    b, l, d = h.shape
    tm = min(TOKEN_TILE, l)
    tok = lambda w: pl.BlockSpec((1, tm, w), lambda j, i: (i, j, 0))
    return pl.pallas_call(
        _merge_kernel,
        out_shape=jax.ShapeDtypeStruct((b, l, d), F32),
        grid=(l // tm, b),
        in_specs=[tok(d), pl.BlockSpec((1, N_MOD, d), lambda j, i: (i, 0, 0)), tok(KW), tok(HY_W), tok(2 * d),
                  _const_spec(w_pa.shape), _const_spec(w_pb.shape), _const_spec(w_out.shape)],
        out_specs=tok(d),
        compiler_params=_params(2),
        name="merge",
    )(h, m3, o_a, o_b, sg, w_pa, w_pb, w_out)


def _grid_pos_embed(n_tokens):
    rows = n_tokens // GRID_W
    quarter = D_MODEL // 4
    omega = 1.0 / (10000.0 ** (jnp.arange(quarter, dtype=F32) / quarter))
    ar = jnp.arange(rows, dtype=F32)[:, None] * omega
    ac = jnp.arange(GRID_W, dtype=F32)[:, None] * omega
    er = jnp.concatenate([jnp.sin(ar), jnp.cos(ar)], axis=-1)
    ec = jnp.concatenate([jnp.sin(ac), jnp.cos(ac)], axis=-1)
    emb = jnp.concatenate([jnp.broadcast_to(er[:, None, :], (rows, GRID_W, D_MODEL // 2)),
                           jnp.broadcast_to(ec[None, :, :], (rows, GRID_W, D_MODEL // 2))], axis=-1)
    return emb.reshape(rows * GRID_W, D_MODEL)


def kernel(x, c, ctx, c_ctx, mod_w, mod_b, ffn_w_gate, ffn_w_up, ffn_w_down, w_in, hgrn_lb_logits,
           hgrn_norm_w, hyena_conv_w, hyena_conv_b, hyena_w1, hyena_b1, hyena_freq1, hyena_w2, hyena_b2,
           hyena_freq2, hyena_w3, hyena_bias, w_proj_a, w_proj_b, w_out, final_norm_w):
    assert mod_w.shape[0] == 1, "single-layer configuration"
    batch, n_lat, d = x.shape
    bf = lambda a: a.astype(BF16)

    c_all = jnp.concatenate([c, c_ctx[None, :]], axis=0)
    c_all = jnp.pad(c_all, ((0, -c_all.shape[0] % 8), (0, 0)))
    m3 = _modulation(c_all, mod_w[0], mod_b[0][None, :]).reshape(c_all.shape[0], N_MOD, d)

    lb = jnp.cumsum(jax.nn.softmax(hgrn_lb_logits.astype(F32), axis=0), axis=0)[0]
    wg1, wu1, wd1 = bf(ffn_w_gate[0, 0]), bf(ffn_w_up[0, 0]), bf(ffn_w_down[0, 0])
    wg2, wu2, wd2 = bf(ffn_w_gate[0, 1]), bf(ffn_w_up[0, 1]), bf(ffn_w_down[0, 1])
    w_in_b = bf(w_in[0])

    h1 = _half_ffn(x, m3, wg1, wu1, wd1, mod_base=0, pos=_grid_pos_embed(n_lat))
    hc1 = _half_ffn(ctx, m3, wg1, wu1, wd1, mod_base=0, mod_row=batch)

    vc, lffc, lfbc = _input_proj(hc1, m3, w_in_b, lb, mod_row=batch)
    v, lff, lfb, q, g, hy, sg = _input_proj(h1, m3, w_in_b, lb)
    o_a = _hgrn2(q, v, lff, lfb, g, vc, lffc, lfbc, hgrn_norm_w[0][None, :])

    taps, mass = _hyena_filters(n_lat, hyena_w1[0], hyena_b1[0], hyena_freq1[0], hyena_w2[0], hyena_b2[0],
                                hyena_freq2[0], hyena_w3[0])
    taps = taps.reshape(n_lat, 2, 2, HY_W)
    mass = mass.reshape(2, 2, HY_W)
    nrm = (mass[:, 0] + mass[:, 1] + HYENA_L1_EPS).reshape(1, 2 * HY_W)
    kk = jnp.concatenate([taps[:, :, 0].reshape(n_lat, 2 * HY_W),
                          -jnp.flip(taps[:, :, 1], axis=0).reshape(n_lat, 2 * HY_W)], axis=-1)
    fwd, inv = _odd_dft_matrices(n_lat)
    ktab = _filter_spectrum(fwd, bf(kk), nrm)
    o_b = _hyena(hy, hyena_conv_w[0], hyena_conv_b[0][None, :], hyena_bias[0], fwd, inv, ktab)

    h2 = _merge(h1, m3, o_a, o_b, sg, bf(w_proj_a[0]), bf(w_proj_b[0]), bf(w_out[0]))

    return _half_ffn(h2, m3, wg2, wu2, wd2, mod_base=6, final_norm_w=final_norm_w[None, :])
```

```python
import functools
import math

import jax
import jax.numpy as jnp
from jax import lax
from jax.experimental import pallas as pl
from jax.experimental.pallas import tpu as pltpu

F32 = jnp.float32
BF16 = jnp.bfloat16
HIGHEST = lax.Precision.HIGHEST

D_MODEL = 1024
GRID_W = 64
HEADS = 4
HEAD_DIM = 128
KW = HEADS * HEAD_DIM
HY_W = 512
D_FF = 2816
N_MOD = 9
RMS_EPS = 1e-6
HYENA_EMB = 33
HYENA_BANDS = (HYENA_EMB - 1) // 2
HYENA_FFN = 64
HYENA_FAST_DECAY = 0.3
HYENA_SLOW_DECAY = 1.5
HYENA_TARGET = 1e-2
HYENA_SHIFT = 0.05
HYENA_L1_EPS = 1e-6

COL_V = 0
COL_FFW = COL_V + KW
COL_FBW = COL_FFW + KW
COL_Q = COL_FBW + KW
COL_G = COL_Q + KW
COL_HY = COL_G + KW
COL_MERGE = COL_HY + 3 * HY_W
IN_COLS = COL_MERGE + 2 * D_MODEL

V7X_VMEM_BYTES = 64 * 1024 * 1024
VMEM_LIMIT = V7X_VMEM_BYTES - 8 * 1024 * 1024

TOKEN_TILE = 512
FF_CHUNK = 256
SCAN_CHUNK = 64
SCAN_UNROLL = 2
FREQ_TILE = 256
FILT_TILE = 256
FEAT_PAD = 128


def _const_spec(shape):
    nd = len(shape)
    return pl.BlockSpec(shape, lambda *_: (0,) * nd, pipeline_mode=pl.Buffered(1))


def _params(n_grid):
    return pltpu.CompilerParams(dimension_semantics=("arbitrary",) * n_grid, vmem_limit_bytes=VMEM_LIMIT)


def _rms(x):
    return x * lax.rsqrt(jnp.mean(x * x, axis=-1, keepdims=True) + RMS_EPS)


def _norm_mod(h, shift, scale):
    return _rms(h) * (1.0 + scale) + shift


def _dot(a, b):
    return jnp.dot(a, b, preferred_element_type=F32)


def _mod_kernel(c_ref, w_ref, b_ref, o_ref):
    c = c_ref[...]
    a = c * jax.nn.sigmoid(c)
    o_ref[...] = jnp.dot(a, w_ref[...], precision=HIGHEST, preferred_element_type=F32) + b_ref[...]


def _modulation(c_all, mod_w, mod_b):
    rows = c_all.shape[0]
    tn = D_MODEL
    return pl.pallas_call(
        _mod_kernel,
        out_shape=jax.ShapeDtypeStruct((rows, N_MOD * D_MODEL), F32),
        grid=(N_MOD,),
        in_specs=[
            pl.BlockSpec((rows, D_MODEL), lambda j: (0, 0)),
            pl.BlockSpec((D_MODEL, tn), lambda j: (0, j)),
            pl.BlockSpec((1, tn), lambda j: (0, j)),
        ],
        out_specs=pl.BlockSpec((rows, tn), lambda j: (0, j)),
        compiler_params=_params(1),
        name="mod",
    )(c_all, mod_w, mod_b)


def _ffn_kernel(*refs, mod_base, add_pos, final_norm):
    refs = list(refs)
    h_ref = refs.pop(0)
    pos_ref = refs.pop(0) if add_pos else None
    m_ref, wg_ref, wu_ref, wd_ref = refs[:4]
    refs = refs[4:]
    fnw_ref = refs.pop(0) if final_norm else None
    o_ref = refs.pop(0)

    h = h_ref[0]
    if add_pos:
        h = h + pos_ref[...]
    shift = m_ref[0, mod_base:mod_base + 1, :]
    scale = m_ref[0, mod_base + 1:mod_base + 2, :]
    gate = m_ref[0, mod_base + 2:mod_base + 3, :]
    nb = _norm_mod(h, shift, scale).astype(BF16)
    acc = jnp.zeros(h.shape, F32)
    for c in range(D_FF // FF_CHUNK):
        sl = slice(c * FF_CHUNK, (c + 1) * FF_CHUNK)
        g = _dot(nb, wg_ref[:, sl])
        u = _dot(nb, wu_ref[:, sl])
        a = (g * jax.nn.sigmoid(g) * u).astype(BF16)
        acc = acc + _dot(a, wd_ref[sl, :])
    out = h + 0.5 * gate * acc
    if final_norm:
        out = _rms(out) * fnw_ref[...]
    o_ref[0] = out


def _half_ffn(h, m3, wg, wu, wd, *, mod_base, mod_row=None, pos=None, final_norm_w=None):
    b, l, d = h.shape
    tm = min(TOKEN_TILE, l)
    nt = l // tm
    row_map = (lambda j, i: (i, 0, 0)) if mod_row is None else (lambda j, i: (mod_row, 0, 0))
    in_specs = [pl.BlockSpec((1, tm, d), lambda j, i: (i, j, 0))]
    args = [h]
    if pos is not None:
        in_specs.append(pl.BlockSpec((tm, d), lambda j, i: (j, 0)))
        args.append(pos)
    in_specs += [
        pl.BlockSpec((1, N_MOD, d), row_map),
        _const_spec(wg.shape), _const_spec(wu.shape), _const_spec(wd.shape),
    ]
    args += [m3, wg, wu, wd]
    if final_norm_w is not None:
        in_specs.append(_const_spec(final_norm_w.shape))
        args.append(final_norm_w)
    kern = functools.partial(_ffn_kernel, mod_base=mod_base, add_pos=pos is not None,
                             final_norm=final_norm_w is not None)
    return pl.pallas_call(
        kern,
        out_shape=jax.ShapeDtypeStruct((b, l, d), F32),
        grid=(nt, b),
        in_specs=in_specs,
        out_specs=pl.BlockSpec((1, tm, d), lambda j, i: (i, j, 0)),
        compiler_params=_params(2),
        name="ffn",
    )(*args)


def _log2_forget(z, lb):
    return jnp.log2(lb + (1.0 - lb) * jax.nn.sigmoid(z))


def _proj_kernel(h_ref, m_ref, w_ref, lb_ref, *out_refs, full):
    h = h_ref[0]
    nb = _norm_mod(h, m_ref[0, 3:4, :], m_ref[0, 4:5, :]).astype(BF16)

    def proj(c0):
        return _dot(nb, w_ref[:, c0:c0 + KW])

    v_ref, lff_ref, lfb_ref = out_refs[:3]
    v_ref[0] = proj(COL_V).astype(BF16)
    lff_ref[0] = _log2_forget(proj(COL_FFW), lb_ref[0:1, :])
    lfb_ref[0] = _log2_forget(proj(COL_FBW), lb_ref[1:2, :])
    if full:
        q_ref, g_ref, hy_ref, mg_ref = out_refs[3:]
        zq = proj(COL_Q)
        q_ref[0] = (zq * jax.nn.sigmoid(zq)).astype(BF16)
        zg = proj(COL_G)
        g_ref[0] = (zg * jax.nn.sigmoid(zg)).astype(BF16)
        for k in range(3 * HY_W // KW):
            hy_ref[0, :, k * KW:(k + 1) * KW] = proj(COL_HY + k * KW).astype(BF16)
        for k in range(2 * D_MODEL // KW):
            mg_ref[0, :, k * KW:(k + 1) * KW] = jax.nn.sigmoid(proj(COL_MERGE + k * KW)).astype(BF16)


def _input_proj(h, m3, w_in, lb, *, mod_row=None):
    b, l, d = h.shape
    full = mod_row is None
    tm = min(TOKEN_TILE, l)
    nt = l // tm
    row_map = (lambda j, i: (i, 0, 0)) if full else (lambda j, i: (mod_row, 0, 0))
    ncols = IN_COLS if full else COL_Q
    tok = lambda w: pl.BlockSpec((1, tm, w), lambda j, i: (i, j, 0))
    shapes = [(KW, BF16), (KW, F32), (KW, F32)]
    if full:
        shapes += [(KW, BF16), (KW, BF16), (3 * HY_W, BF16), (2 * D_MODEL, BF16)]
    return pl.pallas_call(
        functools.partial(_proj_kernel, full=full),
        out_shape=[jax.ShapeDtypeStruct((b, l, w), dt) for w, dt in shapes],
        grid=(nt, b),
        in_specs=[
            tok(d),
            pl.BlockSpec((1, N_MOD, d), row_map),
            pl.BlockSpec((d, ncols), lambda j, i: (0, 0), pipeline_mode=pl.Buffered(1)),
            _const_spec(lb.shape),
        ],
        out_specs=[tok(w) for w, _ in shapes],
        compiler_params=_params(2),
        name="proj" if full else "proj_ctx",
    )(h, m3, w_in, lb)


def _split2(x):
    hi = x.astype(BF16)
    lo = (x - hi.astype(F32)).astype(BF16)
    return hi, lo


def _dot_nt(a, b):
    return lax.dot_general(a, b, (((1,), (1,)), ((), ())), preferred_element_type=F32)


def _scan_kernel(q_ref, v_ref, lff_ref, lfb_ref, g_ref, vc_ref, lffc_ref, lfbc_ref, nw_ref,
                 o_ref, oacc_ref, vt_ref, vtc_ref, st_ref):
    c = SCAN_CHUNK
    n_lat = q_ref.shape[1] // c
    n_ctx = vc_ref.shape[1] // c
    row = lax.broadcasted_iota(jnp.int32, (c, c), 0)
    col = lax.broadcasted_iota(jnp.int32, (c, c), 1)
    lower = row >= col
    upper = row <= col
    tri_l = jnp.where(lower, 1.0, 0.0).astype(BF16)
    tri_u = jnp.where(upper, 1.0, 0.0).astype(BF16)

    for j in range(n_lat):
        vt_ref[j] = v_ref[0, j * c:(j + 1) * c, :].astype(F32).T.astype(BF16)
    for j in range(n_ctx):
        vtc_ref[j] = vc_ref[0, j * c:(j + 1) * c, :].astype(F32).T.astype(BF16)

    def chunks(items):
        heads = [slice(hd * HEAD_DIM, (hd + 1) * HEAD_DIM) for hd in range(HEADS)]
        cfg = [(tri_l, lower, c // 2 - 1, c - 1) if d == 0 else (tri_u, upper, c // 2, 0)
               for d, *_ in items]
        split = [_split2(lf) for _, lf, *_ in items]
        cum = [_dot(tri, hi) + _dot(tri, lo)
               for (tri, *_), (hi, lo) in zip(cfg, split)]
        kd, dec, qm, qd, km = [], [], [], [], []
        for (_, lf, _, _, q), (_, _, ref_i, tot_i), b in zip(items, cfg, cum):
            k = 1.0 - jnp.exp2(lf)
            tot = b[tot_i:tot_i + 1, :]
            kd.append((k * jnp.exp2(tot - b)).astype(BF16))
            dec.append(jnp.exp2(tot))
            if q is not None:
                ref = b[ref_i:ref_i + 1, :]
                qf = q.astype(F32)
                qm.append((qf * jnp.exp2(b - ref)).astype(BF16))
                qd.append((qf * jnp.exp2(b)).astype(BF16))
                km.append((k * jnp.exp2(ref - b)).astype(BF16))
            else:
                qm.append(None)
                qd.append(None)
                km.append(None)
        scores = [[_dot_nt(qm[n][:, hs], km[n][:, hs]) for hs in heads] if qm[n] is not None else None
                  for n in range(len(items))]
        grow = [[_dot(items[n][2][hs, :], kd[n][:, hs]) for hs in heads] for n in range(len(items))]
        state = {d: [st_ref[d, hd] for hd in range(HEADS)] for d in {d for d, *_ in items}}
        carry = []
        for n, (d, *_) in enumerate(items):
            carry.append([_dot_nt(qd[n][:, hs], state[d][hd].astype(BF16)) for hd, hs in enumerate(heads)]
                         if qm[n] is not None else None)
            state[d] = [state[d][hd] * dec[n][:, hs] + grow[n][hd] for hd, hs in enumerate(heads)]
        for d, sts in state.items():
            for hd in range(HEADS):
                st_ref[d, hd] = sts[hd]
        outs = []
        for n, (_, _, _, v, q) in enumerate(items):
            if q is None:
                outs.append(None)
                continue
            mask = cfg[n][1]
            o = [_dot(jnp.where(mask, scores[n][hd], 0.0).astype(BF16), v[:, hs]) + carry[n][hd]
                 for hd, hs in enumerate(heads)]
            outs.append(jnp.concatenate(o, axis=-1))
        return outs

    st_ref[...] = jnp.zeros(st_ref.shape, F32)
    chunks([(0, lffc_ref[0, j * c:(j + 1) * c, :], vtc_ref[j], None, None) for j in range(n_ctx)]
           + [(1, lfbc_ref[0, j * c:(j + 1) * c, :], vtc_ref[j], None, None) for j in reversed(range(n_ctx))])

    nw = nw_ref[...]

    def finish(rows, o):
        normed = [_rms(o[:, hd * HEAD_DIM:(hd + 1) * HEAD_DIM]) * nw for hd in range(HEADS)]
        o_ref[0, rows, :] = (jnp.concatenate(normed, axis=-1) * g_ref[0, rows, :].astype(F32)).astype(BF16)

    unroll = SCAN_UNROLL

    def step(j, final):
        idx = [unroll * j + u for u in range(unroll)] + [n_lat - 1 - (unroll * j + u) for u in range(unroll)]
        rows = [pl.ds(pl.multiple_of(i * c, c), c) for i in idx]
        outs = chunks([(0 if n < unroll else 1, (lff_ref if n < unroll else lfb_ref)[0, r, :], vt_ref[i],
                        v_ref[0, r, :], q_ref[0, r, :]) for n, (i, r) in enumerate(zip(idx, rows))])
        for r, o in zip(rows, outs):
            if final:
                finish(r, oacc_ref[r, :] + o)
            else:
                oacc_ref[r, :] = o

    def first_half(j, carry):
        step(j, False)
        return carry

    def second_half(j, carry):
        step(j, True)
        return carry

    n_steps = n_lat // unroll
    lax.fori_loop(0, n_steps // 2, first_half, 0)
    lax.fori_loop(n_steps // 2, n_steps, second_half, 0)


def _hgrn2(q, v, lff, lfb, g, vc, lffc, lfbc, norm_w):
    b, l, _ = q.shape
    lc = vc.shape[1]
    c = SCAN_CHUNK
    seq = lambda n: pl.BlockSpec((1, n, KW), lambda i: (i, 0, 0))
    return pl.pallas_call(
        _scan_kernel,
        out_shape=jax.ShapeDtypeStruct((b, l, KW), BF16),
        grid=(b,),
        in_specs=[seq(l), seq(l), seq(l), seq(l), seq(l), seq(lc), seq(lc), seq(lc),
                  _const_spec(norm_w.shape)],
        out_specs=seq(l),
        scratch_shapes=[pltpu.VMEM((l, KW), F32), pltpu.VMEM((l // c, KW, c), BF16),
                        pltpu.VMEM((lc // c, KW, c), BF16), pltpu.VMEM((2, HEADS, HEAD_DIM, HEAD_DIM), F32)],
        compiler_params=_params(1),
        name="scan",
    )(q, v, lff, lfb, g, vc, lffc, lfbc, norm_w)


def _filter_kernel(z_ref, w1_ref, b1_ref, f1_ref, w2_ref, b2_ref, f2_ref, w3_ref, win_ref, h_ref, mass_ref):
    hp = dict(precision=HIGHEST, preferred_element_type=F32)
    h = jnp.sin(f1_ref[...] * (jnp.dot(z_ref[...], w1_ref[...], **hp) + b1_ref[...]))
    h = jnp.sin(f2_ref[...] * (jnp.dot(h, w2_ref[...], **hp) + b2_ref[...]))
    h = jnp.dot(h, w3_ref[...], **hp)
    win = win_ref[...]
    n_groups = h.shape[1] // HY_W
    h = jnp.concatenate([h[:, k * HY_W:(k + 1) * HY_W] * win for k in range(n_groups)], axis=-1)
    h_ref[...] = h

    @pl.when(pl.program_id(0) == 0)
    def _():
        mass_ref[...] = jnp.zeros(mass_ref.shape, F32)

    mass_ref[...] += jnp.sum(jnp.abs(h), axis=0, keepdims=True)


def _hyena_filters(l, w1, b1, fr1, w2, b2, fr2, w3):
    p = jnp.arange(l, dtype=F32)
    t = p / (l - 1)
    w = 2.0 * math.pi * p / l
    f = jnp.linspace(1e-4, HYENA_BANDS - 1, HYENA_BANDS, dtype=F32)
    ang = w[:, None] * f[None, :]
    z = jnp.concatenate([t[:, None], jnp.cos(ang), -jnp.sin(ang)], axis=-1)
    max_decay = math.log(HYENA_TARGET) / HYENA_FAST_DECAY
    min_decay = math.log(HYENA_TARGET) / HYENA_SLOW_DECAY
    deltas = jnp.abs(jnp.linspace(min_decay, max_decay, HY_W, dtype=F32))
    window = jnp.exp(-t[:, None] * deltas[None, :]) + HYENA_SHIFT

    pad_c = lambda a, n: jnp.pad(a, ((0, 0), (0, n - a.shape[1])))
    pad_r = lambda a, n: jnp.pad(a, ((0, n - a.shape[0]), (0, 0)))
    z = pad_c(z, FEAT_PAD)
    w1p = pad_c(pad_r(w1, FEAT_PAD), FEAT_PAD)
    w2p = pad_c(pad_r(w2, FEAT_PAD), FEAT_PAD)
    w3p = pad_r(w3, FEAT_PAD)
    vec = lambda a: pad_c(a[None, :], FEAT_PAD)
    n_out = w3.shape[1]
    tl = FILT_TILE
    return pl.pallas_call(
        _filter_kernel,
        out_shape=[jax.ShapeDtypeStruct((l, n_out), F32), jax.ShapeDtypeStruct((1, n_out), F32)],
        grid=(l // tl,),
        in_specs=[
            pl.BlockSpec((tl, FEAT_PAD), lambda i: (i, 0)),
            _const_spec(w1p.shape), _const_spec((1, FEAT_PAD)), _const_spec((1, FEAT_PAD)),
            _const_spec(w2p.shape), _const_spec((1, FEAT_PAD)), _const_spec((1, FEAT_PAD)),
            _const_spec(w3p.shape),
            pl.BlockSpec((tl, HY_W), lambda i: (i, 0)),
        ],
        out_specs=[pl.BlockSpec((tl, n_out), lambda i: (i, 0)), pl.BlockSpec((1, n_out), lambda i: (0, 0))],
        compiler_params=_params(1),
        name="filt",
    )(z, w1p, vec(b1), vec(fr1), w2p, vec(b2), vec(fr2), w3p, window)


def _odd_dft_matrices(l):
    n_ang = 4 * l
    lo_n = 64
    f2 = 2 * jnp.arange(l, dtype=jnp.int32) + 1
    theta = 2.0 * math.pi / n_ang

    def cis(idx):
        a = (idx % n_ang).astype(F32) * theta
        return jnp.cos(a), jnp.sin(a)

    hr, hi = cis(f2[:, None] * (lo_n * jnp.arange(l // lo_n, dtype=jnp.int32))[None, :])
    lr, li = cis(f2[:, None] * jnp.arange(lo_n, dtype=jnp.int32)[None, :])
    cos = (hr[:, :, None] * lr[:, None, :] - hi[:, :, None] * li[:, None, :]).reshape(l, l)
    sin = (hr[:, :, None] * li[:, None, :] + hi[:, :, None] * lr[:, None, :]).reshape(l, l)
    nf = l // FREQ_TILE
    fwd = jnp.stack([cos.reshape(nf, FREQ_TILE, l), sin.reshape(nf, FREQ_TILE, l)], axis=1)
    fwd = fwd.reshape(2 * l, l).astype(BF16)
    return fwd, fwd.T


def _kdft_kernel(fwd_ref, kk_ref, nrm_ref, o_ref):
    tf = FREQ_TILE
    half = kk_ref.shape[1] // 2
    r = _dot(fwd_ref[...], kk_ref[...])
    a_c, a_s = r[:tf, :half], r[tf:, :half]
    b_c, b_s = r[:tf, half:], r[tf:, half:]
    f = pl.program_id(0) * tf + lax.broadcasted_iota(jnp.int32, (tf, 1), 0)
    sgn = jnp.where(f % 2 == 0, 1.0, -1.0)
    scale = (1.0 / kk_ref.shape[0]) / nrm_ref[...]
    o_ref[0, 0] = (a_c - sgn * b_s) * scale
    o_ref[0, 1] = (-a_s - sgn * b_c) * scale


def _filter_spectrum(fwd, kk, nrm):
    l = kk.shape[0]
    tf = FREQ_TILE
    half = kk.shape[1] // 2
    return pl.pallas_call(
        _kdft_kernel,
        out_shape=jax.ShapeDtypeStruct((l // tf, 2, tf, half), F32),
        grid=(l // tf,),
        in_specs=[pl.BlockSpec((2 * tf, l), lambda i: (i, 0)), _const_spec(kk.shape), _const_spec(nrm.shape)],
        out_specs=pl.BlockSpec((1, 2, tf, half), lambda i: (i, 0, 0, 0)),
        compiler_params=_params(1),
        name="kdft",
    )(fwd, kk, nrm)


def _short_conv(x_ref, cw_ref, cb_ref, k):
    x = x_ref[0].astype(F32)
    l = x.shape[0]
    cs = slice(k * HY_W, (k + 1) * HY_W)
    t = lax.broadcasted_iota(jnp.int32, (l, 1), 0)
    prev = jnp.where(t == 0, 0.0, pltpu.roll(x, 1, 0))
    nxt = jnp.where(t == l - 1, 0.0, pltpu.roll(x, l - 1, 0))
    return cb_ref[0:1, cs] + prev * cw_ref[0:1, cs] + x * cw_ref[1:2, cs] + nxt * cw_ref[2:3, cs]


def _hyena_kernel(hv_ref, hx1_ref, hx2_ref, cw_ref, cb_ref, hb_ref, fwd_ref, inv_ref, kt_ref,
                  o_ref, u_ref, acc_ref):
    order = pl.program_id(1)
    i = pl.program_id(2)
    last = pl.num_programs(2) - 1
    tf = FREQ_TILE

    @pl.when((order == 0) & (i == 0))
    def _():
        v = _short_conv(hv_ref, cw_ref, cb_ref, 0)
        u_ref[...] = v.astype(BF16)
        acc_ref[...] = hb_ref[0:1, :] * v

    u = _dot(fwd_ref[...], u_ref[...])
    uc, us = u[:tf], u[tf:]
    kr, ki = kt_ref[0, 0], kt_ref[0, 1]
    y = jnp.concatenate([kr * uc + ki * us, kr * us - ki * uc], axis=0).astype(BF16)
    acc_ref[...] += _dot(inv_ref[...], y)

    @pl.when((order == 0) & (i == last))
    def _():
        z1 = _short_conv(hx1_ref, cw_ref, cb_ref, 1) * acc_ref[...]
        u_ref[...] = z1.astype(BF16)
        acc_ref[...] = hb_ref[1:2, :] * z1

    @pl.when((order == 1) & (i == last))
    def _():
        o_ref[0] = (_short_conv(hx2_ref, cw_ref, cb_ref, 2) * acc_ref[...]).astype(BF16)


def _hyena(hy, conv_w, conv_b, hy_bias, fwd, inv, ktab):
    b, l, _ = hy.shape
    tf = FREQ_TILE
    nf = l // tf
    col = lambda k: pl.BlockSpec((1, l, HY_W), lambda bi, o, i: (bi, 0, k))
    return pl.pallas_call(
        _hyena_kernel,
        out_shape=jax.ShapeDtypeStruct((b, l, HY_W), BF16),
        grid=(b, 2, nf),
        in_specs=[
            col(0), col(1), col(2),
            _const_spec(conv_w.shape), _const_spec(conv_b.shape), _const_spec(hy_bias.shape),
            pl.BlockSpec((2 * tf, l), lambda bi, o, i: (i, 0)),
            pl.BlockSpec((l, 2 * tf), lambda bi, o, i: (0, i)),
            pl.BlockSpec((1, 2, tf, HY_W), lambda bi, o, i: (i, 0, 0, o)),
        ],
        out_specs=pl.BlockSpec((1, l, HY_W), lambda bi, o, i: (bi, 0, 0)),
        scratch_shapes=[pltpu.VMEM((l, HY_W), BF16), pltpu.VMEM((l, HY_W), F32)],
        compiler_params=_params(3),
        name="hyena",
    )(hy, hy, hy, conv_w, conv_b, hy_bias, fwd, inv, ktab)


def _merge_kernel(h_ref, m_ref, oa_ref, ob_ref, sg_ref, wpa_ref, wpb_ref, wo_ref, o_ref):
    d = h_ref.shape[2]
    ya = _dot(oa_ref[0], wpa_ref[...])
    yb = _dot(ob_ref[0], wpb_ref[...])
    y = sg_ref[0, :, :d].astype(F32) * ya + sg_ref[0, :, d:].astype(F32) * yb
    out = _dot(y.astype(BF16), wo_ref[...])
    o_ref[0] = h_ref[0] + m_ref[0, 5:6, :] * out


def _merge(h, m3, o_a, o_b, sg, w_pa, w_pb, w_out):
    b, l, d = h.shape
    tm = min(TOKEN_TILE, l)
    tok = lambda w: pl.BlockSpec((1, tm, w), lambda j, i: (i, j, 0))
    return pl.pallas_call(
        _merge_kernel,
        out_shape=jax.ShapeDtypeStruct((b, l, d), F32),
        grid=(l // tm, b),
        in_specs=[tok(d), pl.BlockSpec((1, N_MOD, d), lambda j, i: (i, 0, 0)), tok(KW), tok(HY_W), tok(2 * d),
                  _const_spec(w_pa.shape), _const_spec(w_pb.shape), _const_spec(w_out.shape)],
        out_specs=tok(d),
        compiler_params=_params(2),
        name="merge",
    )(h, m3, o_a, o_b, sg, w_pa, w_pb, w_out)


def _grid_pos_embed(n_tokens):
    rows = n_tokens // GRID_W
    quarter = D_MODEL // 4
    omega = 1.0 / (10000.0 ** (jnp.arange(quarter, dtype=F32) / quarter))
    ar = jnp.arange(rows, dtype=F32)[:, None] * omega
    ac = jnp.arange(GRID_W, dtype=F32)[:, None] * omega
    er = jnp.concatenate([jnp.sin(ar), jnp.cos(ar)], axis=-1)
    ec = jnp.concatenate([jnp.sin(ac), jnp.cos(ac)], axis=-1)
    emb = jnp.concatenate([jnp.broadcast_to(er[:, None, :], (rows, GRID_W, D_MODEL // 2)),
                           jnp.broadcast_to(ec[None, :, :], (rows, GRID_W, D_MODEL // 2))], axis=-1)
    return emb.reshape(rows * GRID_W, D_MODEL)


def kernel(x, c, ctx, c_ctx, mod_w, mod_b, ffn_w_gate, ffn_w_up, ffn_w_down, w_in, hgrn_lb_logits,
           hgrn_norm_w, hyena_conv_w, hyena_conv_b, hyena_w1, hyena_b1, hyena_freq1, hyena_w2, hyena_b2,
           hyena_freq2, hyena_w3, hyena_bias, w_proj_a, w_proj_b, w_out, final_norm_w):
    assert mod_w.shape[0] == 1, "single-layer configuration"
    batch, n_lat, d = x.shape
    bf = lambda a: a.astype(BF16)

    c_all = jnp.concatenate([c, c_ctx[None, :]], axis=0)
    c_all = jnp.pad(c_all, ((0, -c_all.shape[0] % 8), (0, 0)))
    m3 = _modulation(c_all, mod_w[0], mod_b[0][None, :]).reshape(c_all.shape[0], N_MOD, d)

    lb = jnp.cumsum(jax.nn.softmax(hgrn_lb_logits.astype(F32), axis=0), axis=0)[0]
    wg1, wu1, wd1 = bf(ffn_w_gate[0, 0]), bf(ffn_w_up[0, 0]), bf(ffn_w_down[0, 0])
    wg2, wu2, wd2 = bf(ffn_w_gate[0, 1]), bf(ffn_w_up[0, 1]), bf(ffn_w_down[0, 1])
    w_in_b = bf(w_in[0])

    h1 = _half_ffn(x, m3, wg1, wu1, wd1, mod_base=0, pos=_grid_pos_embed(n_lat))
    hc1 = _half_ffn(ctx, m3, wg1, wu1, wd1, mod_base=0, mod_row=batch)

    vc, lffc, lfbc = _input_proj(hc1, m3, w_in_b, lb, mod_row=batch)
    v, lff, lfb, q, g, hy, sg = _input_proj(h1, m3, w_in_b, lb)
    o_a = _hgrn2(q, v, lff, lfb, g, vc, lffc, lfbc, hgrn_norm_w[0][None, :])

    taps, mass = _hyena_filters(n_lat, hyena_w1[0], hyena_b1[0], hyena_freq1[0], hyena_w2[0], hyena_b2[0],
                                hyena_freq2[0], hyena_w3[0])
    taps = taps.reshape(n_lat, 2, 2, HY_W)
    mass = mass.reshape(2, 2, HY_W)
    nrm = (mass[:, 0] + mass[:, 1] + HYENA_L1_EPS).reshape(1, 2 * HY_W)
    kk = jnp.concatenate([taps[:, :, 0].reshape(n_lat, 2 * HY_W),
                          -jnp.flip(taps[:, :, 1], axis=0).reshape(n_lat, 2 * HY_W)], axis=-1)
    fwd, inv = _odd_dft_matrices(n_lat)
    ktab = _filter_spectrum(fwd, bf(kk), nrm)
    o_b = _hyena(hy, hyena_conv_w[0], hyena_conv_b[0][None, :], hyena_bias[0], fwd, inv, ktab)

    h2 = _merge(h1, m3, o_a, o_b, sg, bf(w_proj_a[0]), bf(w_proj_b[0]), bf(w_out[0]))

    return _half_ffn(h2, m3, wg2, wu2, wd2, mod_base=6, final_norm_w=final_norm_w[None, :])
```

```python
import functools
import math

import jax
import jax.numpy as jnp
from jax import lax
from jax.experimental import pallas as pl
from jax.experimental.pallas import tpu as pltpu

F32 = jnp.float32
BF16 = jnp.bfloat16
HIGHEST = lax.Precision.HIGHEST

D_MODEL = 1024
GRID_W = 64
HEADS = 4
HEAD_DIM = 128
KW = HEADS * HEAD_DIM
HY_W = 512
D_FF = 2816
N_MOD = 9
RMS_EPS = 1e-6
HYENA_EMB = 33
HYENA_BANDS = (HYENA_EMB - 1) // 2
HYENA_FFN = 64
HYENA_FAST_DECAY = 0.3
HYENA_SLOW_DECAY = 1.5
HYENA_TARGET = 1e-2
HYENA_SHIFT = 0.05
HYENA_L1_EPS = 1e-6

COL_V = 0
COL_FFW = COL_V + KW
COL_FBW = COL_FFW + KW
COL_Q = COL_FBW + KW
COL_G = COL_Q + KW
COL_HY = COL_G + KW
COL_MERGE = COL_HY + 3 * HY_W
IN_COLS = COL_MERGE + 2 * D_MODEL

V7X_VMEM_BYTES = 64 * 1024 * 1024
VMEM_LIMIT = V7X_VMEM_BYTES - 8 * 1024 * 1024

TOKEN_TILE = 512
FF_CHUNK = 256
SCAN_CHUNK = 64
SCAN_UNROLL = 2
FREQ_TILE = 512
FILT_TILE = 256
FEAT_PAD = 128
LANES = 128


def _const_spec(shape):
    nd = len(shape)
    return pl.BlockSpec(shape, lambda *_: (0,) * nd, pipeline_mode=pl.Buffered(1))


def _params(n_grid):
    return pltpu.CompilerParams(dimension_semantics=("arbitrary",) * n_grid, vmem_limit_bytes=VMEM_LIMIT)


def _rms(x):
    return x * lax.rsqrt(jnp.mean(x * x, axis=-1, keepdims=True) + RMS_EPS)


def _norm_mod(h, shift, scale):
    return _rms(h) * (1.0 + scale) + shift


def _dot(a, b):
    return jnp.dot(a, b, preferred_element_type=F32)


CAST_BLOCK_BYTES = 4 * 1024 * 1024


def _cast_kernel(x_ref, o_ref):
    o_ref[...] = x_ref[...].astype(BF16)


def _to_bf16(w):
    w2 = w.reshape(-1, w.shape[-1])
    r, c = w2.shape
    tr = next(r // k for k in range(1, r + 1)
              if r % k == 0 and (r // k) % 8 == 0 and (r // k) * c * 4 <= CAST_BLOCK_BYTES)
    out = pl.pallas_call(
        _cast_kernel,
        out_shape=jax.ShapeDtypeStruct((r, c), BF16),
        grid=(r // tr,),
        in_specs=[pl.BlockSpec((tr, c), lambda i: (i, 0))],
        out_specs=pl.BlockSpec((tr, c), lambda i: (i, 0)),
        compiler_params=_params(1),
        name="cast",
    )(w2)
    return out.reshape(w.shape)


def _mod_kernel(c_ref, w_ref, b_ref, o_ref):
    c = c_ref[...]
    a = c * jax.nn.sigmoid(c)
    o_ref[...] = jnp.dot(a, w_ref[...], precision=HIGHEST, preferred_element_type=F32) + b_ref[...]


def _modulation(c_all, mod_w, mod_b):
    rows = c_all.shape[0]
    tn = D_MODEL
    return pl.pallas_call(
        _mod_kernel,
        out_shape=jax.ShapeDtypeStruct((rows, N_MOD * D_MODEL), F32),
        grid=(N_MOD,),
        in_specs=[
            pl.BlockSpec((rows, D_MODEL), lambda j: (0, 0)),
            pl.BlockSpec((D_MODEL, tn), lambda j: (0, j)),
            pl.BlockSpec((1, tn), lambda j: (0, j)),
        ],
        out_specs=pl.BlockSpec((rows, tn), lambda j: (0, j)),
        compiler_params=_params(1),
        name="mod",
    )(c_all, mod_w, mod_b)


def _ffn_kernel(*refs, mod_base, add_pos, mixers, final_norm):
    refs = list(refs)
    h_ref = refs.pop(0)
    pos_ref = refs.pop(0) if add_pos else None
    m_ref = refs.pop(0)
    if mixers:
        oa_ref, ob_ref, sg_ref, wpa_ref, wpb_ref, wo_ref = refs[:6]
        refs = refs[6:]
    wg_ref, wu_ref, wd_ref = refs[:3]
    refs = refs[3:]
    fnw_ref = refs.pop(0) if final_norm else None
    o_ref = refs.pop(0)

    h = h_ref[0]
    if add_pos:
        h = h + pos_ref[...]
    if mixers:
        d = h.shape[1]
        ya = _dot(oa_ref[0], wpa_ref[...])
        yb = _dot(ob_ref[0], wpb_ref[...])
        y = sg_ref[0, :, :d].astype(F32) * ya + sg_ref[0, :, d:].astype(F32) * yb
        h = h + m_ref[0, 5:6, :] * _dot(y.astype(BF16), wo_ref[...])
    shift = m_ref[0, mod_base:mod_base + 1, :]
    scale = m_ref[0, mod_base + 1:mod_base + 2, :]
    gate = m_ref[0, mod_base + 2:mod_base + 3, :]
    nb = _norm_mod(h, shift, scale).astype(BF16)
    acc = jnp.zeros(h.shape, F32)
    for c in range(D_FF // FF_CHUNK):
        sl = slice(c * FF_CHUNK, (c + 1) * FF_CHUNK)
        g = _dot(nb, wg_ref[0, :, sl])
        u = _dot(nb, wu_ref[0, :, sl])
        a = (g * jax.nn.sigmoid(g) * u).astype(BF16)
        acc = acc + _dot(a, wd_ref[0, sl, :])
    out = h + 0.5 * gate * acc
    if final_norm:
        out = _rms(out) * fnw_ref[...]
    o_ref[0] = out


def _half_ffn(h, m3, wg, wu, wd, half, *, mod_base, mod_row=None, pos=None, mixers=None, final_norm_w=None):
    b, l, d = h.shape
    tm = min(TOKEN_TILE, l)
    nt = l // tm
    row_map = (lambda j, i: (i, 0, 0)) if mod_row is None else (lambda j, i: (mod_row, 0, 0))
    tok = lambda w: pl.BlockSpec((1, tm, w), lambda j, i: (i, j, 0))
    in_specs = [tok(d)]
    args = [h]
    if pos is not None:
        in_specs.append(pl.BlockSpec((tm, d), lambda j, i: (j, 0)))
        args.append(pos)
    in_specs.append(pl.BlockSpec((1, N_MOD, d), row_map))
    args.append(m3)
    if mixers is not None:
        in_specs += [tok(a.shape[2]) for a in mixers[:3]] + [_const_spec(w.shape) for w in mixers[3:]]
        args += list(mixers)
    in_specs += [pl.BlockSpec((1,) + w.shape[1:], lambda j, i: (half, 0, 0), pipeline_mode=pl.Buffered(1))
                 for w in (wg, wu, wd)]
    args += [wg, wu, wd]
    if final_norm_w is not None:
        in_specs.append(_const_spec(final_norm_w.shape))
        args.append(final_norm_w)
    kern = functools.partial(_ffn_kernel, mod_base=mod_base, add_pos=pos is not None,
                             mixers=mixers is not None, final_norm=final_norm_w is not None)
    return pl.pallas_call(
        kern,
        out_shape=jax.ShapeDtypeStruct((b, l, d), F32),
        grid=(nt, b),
        in_specs=in_specs,
        out_specs=tok(d),
        compiler_params=_params(2),
        name="ffn_mix" if mixers is not None else "ffn",
    )(*args)


def _log2_forget(z, lb):
    return jnp.log2(lb + (1.0 - lb) * jax.nn.sigmoid(z))


def _proj_kernel(h_ref, m_ref, w_ref, lb_ref, *out_refs, full):
    h = h_ref[0]
    nb = _norm_mod(h, m_ref[0, 3:4, :], m_ref[0, 4:5, :]).astype(BF16)

    def proj(c0):
        return _dot(nb, w_ref[:, c0:c0 + KW])

    v_ref, lff_ref, lfb_ref = out_refs[:3]
    v_ref[0] = proj(COL_V).astype(BF16)
    lff_ref[0] = _log2_forget(proj(COL_FFW), lb_ref[0:1, :])
    lfb_ref[0] = _log2_forget(proj(COL_FBW), lb_ref[1:2, :])
    if full:
        q_ref, g_ref, hy_ref, mg_ref = out_refs[3:]
        zq = proj(COL_Q)
        q_ref[0] = (zq * jax.nn.sigmoid(zq)).astype(BF16)
        zg = proj(COL_G)
        g_ref[0] = (zg * jax.nn.sigmoid(zg)).astype(BF16)
        for k in range(3 * HY_W // KW):
            hy_ref[0, :, k * KW:(k + 1) * KW] = proj(COL_HY + k * KW).astype(BF16)
        for k in range(2 * D_MODEL // KW):
            mg_ref[0, :, k * KW:(k + 1) * KW] = jax.nn.sigmoid(proj(COL_MERGE + k * KW)).astype(BF16)


def _input_proj(h, m3, w_in, lb, *, mod_row=None):
    b, l, d = h.shape
    full = mod_row is None
    tm = min(TOKEN_TILE, l)
    nt = l // tm
    row_map = (lambda j, i: (i, 0, 0)) if full else (lambda j, i: (mod_row, 0, 0))
    ncols = IN_COLS if full else COL_Q
    tok = lambda w: pl.BlockSpec((1, tm, w), lambda j, i: (i, j, 0))
    shapes = [(KW, BF16), (KW, F32), (KW, F32)]
    if full:
        shapes += [(KW, BF16), (KW, BF16), (3 * HY_W, BF16), (2 * D_MODEL, BF16)]
    return pl.pallas_call(
        functools.partial(_proj_kernel, full=full),
        out_shape=[jax.ShapeDtypeStruct((b, l, w), dt) for w, dt in shapes],
        grid=(nt, b),
        in_specs=[
            tok(d),
            pl.BlockSpec((1, N_MOD, d), row_map),
            pl.BlockSpec((d, ncols), lambda j, i: (0, 0), pipeline_mode=pl.Buffered(1)),
            _const_spec(lb.shape),
        ],
        out_specs=[tok(w) for w, _ in shapes],
        compiler_params=_params(2),
        name="proj" if full else "proj_ctx",
    )(h, m3, w_in, lb)


def _split2(x):
    hi = x.astype(BF16)
    lo = (x - hi.astype(F32)).astype(BF16)
    return hi, lo


def _dot_nt(a, b):
    return lax.dot_general(a, b, (((1,), (1,)), ((), ())), preferred_element_type=F32)


def _scan_kernel(q_ref, v_ref, lff_ref, lfb_ref, g_ref, vc_ref, lffc_ref, lfbc_ref, nw_ref,
                 o_ref, oacc_ref, vt_ref, vtc_ref, st_ref):
    c = SCAN_CHUNK
    n_lat = q_ref.shape[1] // c
    n_ctx = vc_ref.shape[1] // c
    row = lax.broadcasted_iota(jnp.int32, (c, c), 0)
    col = lax.broadcasted_iota(jnp.int32, (c, c), 1)
    lower = row >= col
    upper = row <= col
    tri_l = jnp.where(lower, 1.0, 0.0).astype(BF16)
    tri_u = jnp.where(upper, 1.0, 0.0).astype(BF16)

    for j in range(n_lat):
        vt_ref[j] = v_ref[0, j * c:(j + 1) * c, :].astype(F32).T.astype(BF16)
    for j in range(n_ctx):
        vtc_ref[j] = vc_ref[0, j * c:(j + 1) * c, :].astype(F32).T.astype(BF16)

    def chunks(items):
        heads = [slice(hd * HEAD_DIM, (hd + 1) * HEAD_DIM) for hd in range(HEADS)]
        cfg = [(tri_l, lower, c // 2 - 1, c - 1) if d == 0 else (tri_u, upper, c // 2, 0)
               for d, *_ in items]
        split = [_split2(lf) for _, lf, *_ in items]
        cum = [_dot(tri, hi) + _dot(tri, lo)
               for (tri, *_), (hi, lo) in zip(cfg, split)]
        kd, dec, qm, qd, km = [], [], [], [], []
        for (_, lf, _, _, q), (_, _, ref_i, tot_i), b in zip(items, cfg, cum):
            k = 1.0 - jnp.exp2(lf)
            tot = b[tot_i:tot_i + 1, :]
            kd.append((k * jnp.exp2(tot - b)).astype(BF16))
            dec.append(jnp.exp2(tot))
            if q is not None:
                ref = b[ref_i:ref_i + 1, :]
                qf = q.astype(F32)
                qm.append((qf * jnp.exp2(b - ref)).astype(BF16))
                qd.append((qf * jnp.exp2(b)).astype(BF16))
                km.append((k * jnp.exp2(ref - b)).astype(BF16))
            else:
                qm.append(None)
                qd.append(None)
                km.append(None)
        scores = [[_dot_nt(qm[n][:, hs], km[n][:, hs]) for hs in heads] if qm[n] is not None else None
                  for n in range(len(items))]
        grow = [[_dot(items[n][2][hs, :], kd[n][:, hs]) for hs in heads] for n in range(len(items))]
        state = {d: [st_ref[d, hd] for hd in range(HEADS)] for d in {d for d, *_ in items}}
        carry = []
        for n, (d, *_) in enumerate(items):
            carry.append([_dot_nt(qd[n][:, hs], state[d][hd].astype(BF16)) for hd, hs in enumerate(heads)]
                         if qm[n] is not None else None)
            state[d] = [state[d][hd] * dec[n][:, hs] + grow[n][hd] for hd, hs in enumerate(heads)]
        for d, sts in state.items():
            for hd in range(HEADS):
                st_ref[d, hd] = sts[hd]
        outs = []
        for n, (_, _, _, v, q) in enumerate(items):
            if q is None:
                outs.append(None)
                continue
            mask = cfg[n][1]
            o = [_dot(jnp.where(mask, scores[n][hd], 0.0).astype(BF16), v[:, hs]) + carry[n][hd]
                 for hd, hs in enumerate(heads)]
            outs.append(jnp.concatenate(o, axis=-1))
        return outs

    st_ref[...] = jnp.zeros(st_ref.shape, F32)
    chunks([(0, lffc_ref[0, j * c:(j + 1) * c, :], vtc_ref[j], None, None) for j in range(n_ctx)]
           + [(1, lfbc_ref[0, j * c:(j + 1) * c, :], vtc_ref[j], None, None) for j in reversed(range(n_ctx))])

    nw = nw_ref[...]

    def finish(rows, o):
        normed = [_rms(o[:, hd * HEAD_DIM:(hd + 1) * HEAD_DIM]) * nw for hd in range(HEADS)]
        o_ref[0, rows, :] = (jnp.concatenate(normed, axis=-1) * g_ref[0, rows, :].astype(F32)).astype(BF16)

    unroll = SCAN_UNROLL

    def step(j, final):
        idx = [unroll * j + u for u in range(unroll)] + [n_lat - 1 - (unroll * j + u) for u in range(unroll)]
        rows = [pl.ds(pl.multiple_of(i * c, c), c) for i in idx]
        outs = chunks([(0 if n < unroll else 1, (lff_ref if n < unroll else lfb_ref)[0, r, :], vt_ref[i],
                        v_ref[0, r, :], q_ref[0, r, :]) for n, (i, r) in enumerate(zip(idx, rows))])
        for r, o in zip(rows, outs):
            if final:
                finish(r, oacc_ref[r, :] + o)
            else:
                oacc_ref[r, :] = o

    def first_half(j, carry):
        step(j, False)
        return carry

    def second_half(j, carry):
        step(j, True)
        return carry

    n_steps = n_lat // unroll
    lax.fori_loop(0, n_steps // 2, first_half, 0)
    lax.fori_loop(n_steps // 2, n_steps, second_half, 0)


def _hgrn2(q, v, lff, lfb, g, vc, lffc, lfbc, norm_w):
    b, l, _ = q.shape
    lc = vc.shape[1]
    c = SCAN_CHUNK
    seq = lambda n: pl.BlockSpec((1, n, KW), lambda i: (i, 0, 0))
    return pl.pallas_call(
        _scan_kernel,
        out_shape=jax.ShapeDtypeStruct((b, l, KW), BF16),
        grid=(b,),
        in_specs=[seq(l), seq(l), seq(l), seq(l), seq(l), seq(lc), seq(lc), seq(lc),
                  _const_spec(norm_w.shape)],
        out_specs=seq(l),
        scratch_shapes=[pltpu.VMEM((l, KW), F32), pltpu.VMEM((l // c, KW, c), BF16),
                        pltpu.VMEM((lc // c, KW, c), BF16), pltpu.VMEM((2, HEADS, HEAD_DIM, HEAD_DIM), F32)],
        compiler_params=_params(1),
        name="scan",
    )(q, v, lff, lfb, g, vc, lffc, lfbc, norm_w)


def _filter_kernel(z_ref, w1_ref, b1_ref, f1_ref, w2_ref, b2_ref, f2_ref, w3_ref, win_ref, h_ref, mass_ref):
    hp = dict(precision=HIGHEST, preferred_element_type=F32)
    h = jnp.sin(f1_ref[...] * (jnp.dot(z_ref[...], w1_ref[...], **hp) + b1_ref[...]))
    h = jnp.sin(f2_ref[...] * (jnp.dot(h, w2_ref[...], **hp) + b2_ref[...]))
    h = jnp.dot(h, w3_ref[...], **hp)
    win = win_ref[...]
    n_groups = h.shape[1] // HY_W
    h = jnp.concatenate([h[:, k * HY_W:(k + 1) * HY_W] * win for k in range(n_groups)], axis=-1)
    h_ref[...] = h

    @pl.when(pl.program_id(0) == 0)
    def _():
        mass_ref[...] = jnp.zeros(mass_ref.shape, F32)

    mass_ref[...] += jnp.sum(jnp.abs(h), axis=0, keepdims=True)


def _hyena_filters(l, w1, b1, fr1, w2, b2, fr2, w3):
    p = jnp.arange(l, dtype=F32)
    t = p / (l - 1)
    w = 2.0 * math.pi * p / l
    f = jnp.linspace(1e-4, HYENA_BANDS - 1, HYENA_BANDS, dtype=F32)
    ang = w[:, None] * f[None, :]
    z = jnp.concatenate([t[:, None], jnp.cos(ang), -jnp.sin(ang)], axis=-1)
    max_decay = math.log(HYENA_TARGET) / HYENA_FAST_DECAY
    min_decay = math.log(HYENA_TARGET) / HYENA_SLOW_DECAY
    deltas = jnp.abs(jnp.linspace(min_decay, max_decay, HY_W, dtype=F32))
    window = jnp.exp(-t[:, None] * deltas[None, :]) + HYENA_SHIFT

    pad_c = lambda a, n: jnp.pad(a, ((0, 0), (0, n - a.shape[1])))
    pad_r = lambda a, n: jnp.pad(a, ((0, n - a.shape[0]), (0, 0)))
    z = pad_c(z, FEAT_PAD)
    w1p = pad_c(pad_r(w1, FEAT_PAD), FEAT_PAD)
    w2p = pad_c(pad_r(w2, FEAT_PAD), FEAT_PAD)
    w3p = pad_r(w3, FEAT_PAD)
    vec = lambda a: pad_c(a[None, :], FEAT_PAD)
    n_out = w3.shape[1]
    tl = FILT_TILE
    return pl.pallas_call(
        _filter_kernel,
        out_shape=[jax.ShapeDtypeStruct((l, n_out), F32), jax.ShapeDtypeStruct((1, n_out), F32)],
        grid=(l // tl,),
        in_specs=[
            pl.BlockSpec((tl, FEAT_PAD), lambda i: (i, 0)),
            _const_spec(w1p.shape), _const_spec((1, FEAT_PAD)), _const_spec((1, FEAT_PAD)),
            _const_spec(w2p.shape), _const_spec((1, FEAT_PAD)), _const_spec((1, FEAT_PAD)),
            _const_spec(w3p.shape),
            pl.BlockSpec((tl, HY_W), lambda i: (i, 0)),
        ],
        out_specs=[pl.BlockSpec((tl, n_out), lambda i: (i, 0)), pl.BlockSpec((1, n_out), lambda i: (0, 0))],
        compiler_params=_params(1),
        name="filt",
    )(z, w1p, vec(b1), vec(fr1), w2p, vec(b2), vec(fr2), w3p, window)


def _dft_kernel(ar_ref, ai_ref, br_ref, bi_ref, cr_ref, ci_ref, dr_ref, di_ref, fwd_ref, inv_ref):
    tf = FREQ_TILE
    br, bi = br_ref[...], bi_ref[...]
    for t1 in range(fwd_ref.shape[1] // LANES):
        ar, ai = ar_ref[:, t1:t1 + 1], ai_ref[:, t1:t1 + 1]
        cols = slice(t1 * LANES, (t1 + 1) * LANES)
        fwd_ref[0:tf, cols] = (ar * br - ai * bi).astype(BF16)
        fwd_ref[tf:2 * tf, cols] = (ar * bi + ai * br).astype(BF16)
    dr, di = dr_ref[...], di_ref[...]
    for f1 in range(tf // LANES):
        cr, ci = cr_ref[0, :, f1:f1 + 1], ci_ref[0, :, f1:f1 + 1]
        inv_ref[0, :, f1 * LANES:(f1 + 1) * LANES] = (cr * dr - ci * di).astype(BF16)
        inv_ref[0, :, tf + f1 * LANES:tf + (f1 + 1) * LANES] = (cr * di + ci * dr).astype(BF16)


def _odd_dft_matrices(l):
    n_ang = 4 * l
    theta = 2.0 * math.pi / n_ang
    tf = FREQ_TILE
    nf = l // tf
    n_hi = l // LANES

    def cis(idx):
        a = (idx % n_ang).astype(F32) * theta
        return jnp.cos(a), jnp.sin(a)

    idx = jnp.arange(l, dtype=jnp.int32)
    odd = 2 * idx + 1
    lane = jnp.arange(LANES, dtype=jnp.int32)
    ar, ai = cis(odd[:, None] * (LANES * jnp.arange(n_hi, dtype=jnp.int32))[None, :])
    br, bi = cis(odd[:, None] * lane[None, :])
    g = jnp.arange(n_hi, dtype=jnp.int32).reshape(nf, 1, tf // LANES)
    cr, ci = cis(idx[None, :, None] * (2 * LANES * g))
    dr, di = cis(idx[:, None] * (2 * lane + 1)[None, :])
    row_tab = lambda w: pl.BlockSpec((tf, w), lambda i: (i, 0))
    return pl.pallas_call(
        _dft_kernel,
        out_shape=[jax.ShapeDtypeStruct((2 * l, l), BF16), jax.ShapeDtypeStruct((nf, l, 2 * tf), BF16)],
        grid=(nf,),
        in_specs=[row_tab(n_hi), row_tab(n_hi), row_tab(LANES), row_tab(LANES),
                  pl.BlockSpec((1, l, tf // LANES), lambda i: (i, 0, 0)),
                  pl.BlockSpec((1, l, tf // LANES), lambda i: (i, 0, 0)),
                  _const_spec((l, LANES)), _const_spec((l, LANES))],
        out_specs=[pl.BlockSpec((2 * tf, l), lambda i: (i, 0)), pl.BlockSpec((1, l, 2 * tf), lambda i: (i, 0, 0))],
        compiler_params=_params(1),
        name="dft",
    )(ar, ai, br, bi, cr, ci, dr, di)


def _kdft_kernel(fwd_ref, kk_ref, nrm_ref, o_ref):
    tf = FREQ_TILE
    half = kk_ref.shape[1] // 2
    r = _dot(fwd_ref[...], kk_ref[...])
    a_c, a_s = r[:tf, :half], r[tf:, :half]
    b_c, b_s = r[:tf, half:], r[tf:, half:]
    f = pl.program_id(0) * tf + lax.broadcasted_iota(jnp.int32, (tf, 1), 0)
    sgn = jnp.where(f % 2 == 0, 1.0, -1.0)
    scale = (1.0 / kk_ref.shape[0]) / nrm_ref[...]
    o_ref[0, 0] = (a_c - sgn * b_s) * scale
    o_ref[0, 1] = (-a_s - sgn * b_c) * scale


def _filter_spectrum(fwd, kk, nrm):
    l = kk.shape[0]
    tf = FREQ_TILE
    half = kk.shape[1] // 2
    return pl.pallas_call(
        _kdft_kernel,
        out_shape=jax.ShapeDtypeStruct((l // tf, 2, tf, half), F32),
        grid=(l // tf,),
        in_specs=[pl.BlockSpec((2 * tf, l), lambda i: (i, 0)), _const_spec(kk.shape), _const_spec(nrm.shape)],
        out_specs=pl.BlockSpec((1, 2, tf, half), lambda i: (i, 0, 0, 0)),
        compiler_params=_params(1),
        name="kdft",
    )(fwd, kk, nrm)


def _short_conv(x_ref, cw, cb):
    x = x_ref[0].astype(F32)
    l = x.shape[0]
    t = lax.broadcasted_iota(jnp.int32, (l, 1), 0)
    prev = jnp.where(t == 0, 0.0, pltpu.roll(x, 1, 0))
    nxt = jnp.where(t == l - 1, 0.0, pltpu.roll(x, l - 1, 0))
    return cb + prev * cw[0:1, :] + x * cw[1:2, :] + nxt * cw[2:3, :]


def _hyena_kernel(hv_ref, hx_ref, cw_ref, cb_ref, hb_ref, fwd_ref, inv_ref, kt_ref,
                  o_ref, u_ref, acc_ref):
    order = pl.program_id(1)
    i = pl.program_id(2)
    tf = FREQ_TILE
    l = u_ref.shape[0]
    nf = l // tf
    last = nf - 1

    @pl.when((order == 0) & (i == 0))
    def _():
        v = _short_conv(hv_ref, cw_ref[0], cb_ref[0])
        u_ref[...] = v.astype(BF16)
        acc_ref[...] = hb_ref[0:1, :] * v

    u = _dot(fwd_ref[...], u_ref[...])
    uc, us = u[:tf], u[tf:]
    kr, ki = kt_ref[0, 0], kt_ref[0, 1]
    y = jnp.concatenate([kr * uc + ki * us, kr * us - ki * uc], axis=0).astype(BF16)
    acc_ref[...] += _dot(inv_ref[0], y)

    @pl.when((order == 0) & (i == last))
    def _():
        z1 = _short_conv(hx_ref, cw_ref[1], cb_ref[1]) * acc_ref[...]
        u_ref[...] = z1.astype(BF16)
        acc_ref[...] = hb_ref[1:2, :] * z1

    @pl.when((order == 1) & (i == last))
    def _():
        o_ref[0] = (_short_conv(hx_ref, cw_ref[2], cb_ref[2]) * acc_ref[...]).astype(BF16)


def _hyena(hy, conv_w, conv_b, hy_bias, fwd, inv, ktab):
    b, l, _ = hy.shape
    tf = FREQ_TILE
    nf = l // tf
    cw = conv_w.reshape(conv_w.shape[0], 3, HY_W).transpose(1, 0, 2)
    cb = conv_b.reshape(3, 1, HY_W)
    return pl.pallas_call(
        _hyena_kernel,
        out_shape=jax.ShapeDtypeStruct((b, l, HY_W), BF16),
        grid=(b, 2, nf),
        in_specs=[
            pl.BlockSpec((1, l, HY_W), lambda bi, o, i: (bi, 0, 0)),
            pl.BlockSpec((1, l, HY_W), lambda bi, o, i: (bi, 0, 1 + o)),
            _const_spec(cw.shape), _const_spec(cb.shape), _const_spec(hy_bias.shape),
            pl.BlockSpec((2 * tf, l), lambda bi, o, i: (i, 0)),
            pl.BlockSpec((1, l, 2 * tf), lambda bi, o, i: (i, 0, 0)),
            pl.BlockSpec((1, 2, tf, HY_W), lambda bi, o, i: (i, 0, 0, o)),
        ],
        out_specs=pl.BlockSpec((1, l, HY_W), lambda bi, o, i: (bi, 0, 0)),
        scratch_shapes=[pltpu.VMEM((l, HY_W), BF16), pltpu.VMEM((l, HY_W), F32)],
        compiler_params=_params(3),
        name="hyena",
    )(hy, hy, cw, cb, hy_bias, fwd, inv, ktab)


def _grid_pos_embed(n_tokens):
    rows = n_tokens // GRID_W
    quarter = D_MODEL // 4
    omega = 1.0 / (10000.0 ** (jnp.arange(quarter, dtype=F32) / quarter))
    ar = jnp.arange(rows, dtype=F32)[:, None] * omega
    ac = jnp.arange(GRID_W, dtype=F32)[:, None] * omega
    er = jnp.concatenate([jnp.sin(ar), jnp.cos(ar)], axis=-1)
    ec = jnp.concatenate([jnp.sin(ac), jnp.cos(ac)], axis=-1)
    emb = jnp.concatenate([jnp.broadcast_to(er[:, None, :], (rows, GRID_W, D_MODEL // 2)),
                           jnp.broadcast_to(ec[None, :, :], (rows, GRID_W, D_MODEL // 2))], axis=-1)
    return emb.reshape(rows * GRID_W, D_MODEL)


def kernel(x, c, ctx, c_ctx, mod_w, mod_b, ffn_w_gate, ffn_w_up, ffn_w_down, w_in, hgrn_lb_logits,
           hgrn_norm_w, hyena_conv_w, hyena_conv_b, hyena_w1, hyena_b1, hyena_freq1, hyena_w2, hyena_b2,
           hyena_freq2, hyena_w3, hyena_bias, w_proj_a, w_proj_b, w_out, final_norm_w):
    assert mod_w.shape[0] == 1, "single-layer configuration"
    batch, n_lat, d = x.shape
    bf = lambda a: a.astype(BF16)

    c_all = jnp.concatenate([c, c_ctx[None, :]], axis=0)
    c_all = jnp.pad(c_all, ((0, -c_all.shape[0] % 8), (0, 0)))
    m3 = _modulation(c_all, mod_w[0], mod_b[0][None, :]).reshape(c_all.shape[0], N_MOD, d)

    lb = jnp.cumsum(jax.nn.softmax(hgrn_lb_logits.astype(F32), axis=0), axis=0)[0]
    wg, wu, wd = _to_bf16(ffn_w_gate[0]), _to_bf16(ffn_w_up[0]), _to_bf16(ffn_w_down[0])
    w_in_b = _to_bf16(w_in[0])

    h1 = _half_ffn(x, m3, wg, wu, wd, 0, mod_base=0, pos=_grid_pos_embed(n_lat))
    hc1 = _half_ffn(ctx, m3, wg, wu, wd, 0, mod_base=0, mod_row=batch)

    vc, lffc, lfbc = _input_proj(hc1, m3, w_in_b, lb, mod_row=batch)
    v, lff, lfb, q, g, hy, sg = _input_proj(h1, m3, w_in_b, lb)
    o_a = _hgrn2(q, v, lff, lfb, g, vc, lffc, lfbc, hgrn_norm_w[0][None, :])

    taps, mass = _hyena_filters(n_lat, hyena_w1[0], hyena_b1[0], hyena_freq1[0], hyena_w2[0], hyena_b2[0],
                                hyena_freq2[0], hyena_w3[0])
    taps = taps.reshape(n_lat, 2, 2, HY_W)
    mass = mass.reshape(2, 2, HY_W)
    nrm = (mass[:, 0] + mass[:, 1] + HYENA_L1_EPS).reshape(1, 2 * HY_W)
    kk = jnp.concatenate([taps[:, :, 0].reshape(n_lat, 2 * HY_W),
                          -jnp.flip(taps[:, :, 1], axis=0).reshape(n_lat, 2 * HY_W)], axis=-1)
    fwd, inv = _odd_dft_matrices(n_lat)
    ktab = _filter_spectrum(fwd, bf(kk), nrm)
    o_b = _hyena(hy, hyena_conv_w[0], hyena_conv_b[0], hyena_bias[0], fwd, inv, ktab)

    mixers = (o_a, o_b, sg, _to_bf16(w_proj_a[0]), _to_bf16(w_proj_b[0]), _to_bf16(w_out[0]))
    return _half_ffn(h1, m3, wg, wu, wd, 1, mod_base=6, mixers=mixers, final_norm_w=final_norm_w[None, :])
```

```python
import functools
import math

import jax
import jax.numpy as jnp
from jax import lax
from jax.experimental import pallas as pl
from jax.experimental.pallas import tpu as pltpu

F32 = jnp.float32
BF16 = jnp.bfloat16
HIGHEST = lax.Precision.HIGHEST

D_MODEL = 1024
GRID_W = 64
HEADS = 4
HEAD_DIM = 128
KW = HEADS * HEAD_DIM
HY_W = 512
D_FF = 2816
N_MOD = 9
RMS_EPS = 1e-6
HYENA_EMB = 33
HYENA_BANDS = (HYENA_EMB - 1) // 2
HYENA_FFN = 64
HYENA_FAST_DECAY = 0.3
HYENA_SLOW_DECAY = 1.5
HYENA_TARGET = 1e-2
HYENA_SHIFT = 0.05
HYENA_L1_EPS = 1e-6

COL_V = 0
COL_FFW = COL_V + KW
COL_FBW = COL_FFW + KW
COL_Q = COL_FBW + KW
COL_G = COL_Q + KW
COL_HY = COL_G + KW
COL_MERGE = COL_HY + 3 * HY_W
IN_COLS = COL_MERGE + 2 * D_MODEL

V7X_VMEM_BYTES = 64 * 1024 * 1024
VMEM_LIMIT = V7X_VMEM_BYTES - 8 * 1024 * 1024

TOKEN_TILE = 512
FF_CHUNK = 256
SCAN_CHUNK = 64
SCAN_UNROLL = 2
FREQ_TILE = 512
FILT_TILE = 256
FEAT_PAD = 128
LANES = 128


def _const_spec(shape):
    nd = len(shape)
    return pl.BlockSpec(shape, lambda *_: (0,) * nd, pipeline_mode=pl.Buffered(1))


def _params(n_grid):
    return pltpu.CompilerParams(dimension_semantics=("arbitrary",) * n_grid, vmem_limit_bytes=VMEM_LIMIT)


def _rms(x):
    return x * lax.rsqrt(jnp.mean(x * x, axis=-1, keepdims=True) + RMS_EPS)


def _norm_mod(h, shift, scale):
    return _rms(h) * (1.0 + scale) + shift


def _dot(a, b):
    return jnp.dot(a, b, preferred_element_type=F32)


CAST_BLOCK_BYTES = 4 * 1024 * 1024


def _cast_kernel(x_ref, o_ref):
    o_ref[...] = x_ref[...].astype(BF16)


def _to_bf16(w):
    w2 = w.reshape(-1, w.shape[-1])
    r, c = w2.shape
    tr = next(r // k for k in range(1, r + 1)
              if r % k == 0 and (r // k) % 8 == 0 and (r // k) * c * 4 <= CAST_BLOCK_BYTES)
    out = pl.pallas_call(
        _cast_kernel,
        out_shape=jax.ShapeDtypeStruct((r, c), BF16),
        grid=(r // tr,),
        in_specs=[pl.BlockSpec((tr, c), lambda i: (i, 0))],
        out_specs=pl.BlockSpec((tr, c), lambda i: (i, 0)),
        compiler_params=_params(1),
        name="cast",
    )(w2)
    return out.reshape(w.shape)


def _mod_kernel(c_ref, w_ref, b_ref, o_ref):
    c = c_ref[...]
    a = c * jax.nn.sigmoid(c)
    o_ref[...] = jnp.dot(a, w_ref[...], precision=HIGHEST, preferred_element_type=F32) + b_ref[...]


def _modulation(c_all, mod_w, mod_b):
    rows = c_all.shape[0]
    tn = D_MODEL
    return pl.pallas_call(
        _mod_kernel,
        out_shape=jax.ShapeDtypeStruct((rows, N_MOD * D_MODEL), F32),
        grid=(N_MOD,),
        in_specs=[
            pl.BlockSpec((rows, D_MODEL), lambda j: (0, 0)),
            pl.BlockSpec((D_MODEL, tn), lambda j: (0, j)),
            pl.BlockSpec((1, tn), lambda j: (0, j)),
        ],
        out_specs=pl.BlockSpec((rows, tn), lambda j: (0, j)),
        compiler_params=_params(1),
        name="mod",
    )(c_all, mod_w, mod_b)


def _ffn_kernel(*refs, mod_base, add_pos, mixers, final_norm):
    refs = list(refs)
    h_ref = refs.pop(0)
    pos_ref = refs.pop(0) if add_pos else None
    m_ref = refs.pop(0)
    if mixers:
        oa_ref, ob_ref, sg_ref, wpa_ref, wpb_ref, wo_ref = refs[:6]
        refs = refs[6:]
    wg_ref, wu_ref, wd_ref = refs[:3]
    refs = refs[3:]
    fnw_ref = refs.pop(0) if final_norm else None
    o_ref = refs.pop(0)

    h = h_ref[0]
    if add_pos:
        h = h + pos_ref[...]
    if mixers:
        d = h.shape[1]
        ya = _dot(oa_ref[0], wpa_ref[...])
        yb = _dot(ob_ref[0], wpb_ref[...])
        y = sg_ref[0, :, :d].astype(F32) * ya + sg_ref[0, :, d:].astype(F32) * yb
        h = h + m_ref[0, 5:6, :] * _dot(y.astype(BF16), wo_ref[...])
    shift = m_ref[0, mod_base:mod_base + 1, :]
    scale = m_ref[0, mod_base + 1:mod_base + 2, :]
    gate = m_ref[0, mod_base + 2:mod_base + 3, :]
    nb = _norm_mod(h, shift, scale).astype(BF16)
    acc = jnp.zeros(h.shape, F32)
    for c in range(D_FF // FF_CHUNK):
        sl = slice(c * FF_CHUNK, (c + 1) * FF_CHUNK)
        g = _dot(nb, wg_ref[0, :, sl])
        u = _dot(nb, wu_ref[0, :, sl])
        a = (g * jax.nn.sigmoid(g) * u).astype(BF16)
        acc = acc + _dot(a, wd_ref[0, sl, :])
    out = h + 0.5 * gate * acc
    if final_norm:
        out = _rms(out) * fnw_ref[...]
    o_ref[0] = out


def _half_ffn(h, m3, wg, wu, wd, half, *, mod_base, mod_row=None, pos=None, mixers=None, final_norm_w=None):
    b, l, d = h.shape
    tm = min(TOKEN_TILE, l)
    nt = l // tm
    row_map = (lambda j, i: (i, 0, 0)) if mod_row is None else (lambda j, i: (mod_row, 0, 0))
    tok = lambda w: pl.BlockSpec((1, tm, w), lambda j, i: (i, j, 0))
    in_specs = [tok(d)]
    args = [h]
    if pos is not None:
        in_specs.append(pl.BlockSpec((tm, d), lambda j, i: (j, 0)))
        args.append(pos)
    in_specs.append(pl.BlockSpec((1, N_MOD, d), row_map))
    args.append(m3)
    if mixers is not None:
        in_specs += [tok(a.shape[2]) for a in mixers[:3]] + [_const_spec(w.shape) for w in mixers[3:]]
        args += list(mixers)
    in_specs += [pl.BlockSpec((1,) + w.shape[1:], lambda j, i: (half, 0, 0), pipeline_mode=pl.Buffered(1))
                 for w in (wg, wu, wd)]
    args += [wg, wu, wd]
    if final_norm_w is not None:
        in_specs.append(_const_spec(final_norm_w.shape))
        args.append(final_norm_w)
    kern = functools.partial(_ffn_kernel, mod_base=mod_base, add_pos=pos is not None,
                             mixers=mixers is not None, final_norm=final_norm_w is not None)
    return pl.pallas_call(
        kern,
        out_shape=jax.ShapeDtypeStruct((b, l, d), F32),
        grid=(nt, b),
        in_specs=in_specs,
        out_specs=tok(d),
        compiler_params=_params(2),
        name="ffn_mix" if mixers is not None else "ffn",
    )(*args)


def _log2_forget(z, lb):
    return jnp.log2(lb + (1.0 - lb) * jax.nn.sigmoid(z))


def _proj_kernel(h_ref, m_ref, w_ref, lb_ref, *out_refs, full):
    h = h_ref[0]
    nb = _norm_mod(h, m_ref[0, 3:4, :], m_ref[0, 4:5, :]).astype(BF16)

    def proj(c0):
        return _dot(nb, w_ref[:, c0:c0 + KW])

    v_ref, lff_ref, lfb_ref = out_refs[:3]
    v_ref[0] = proj(COL_V).astype(BF16)
    lff_ref[0] = _log2_forget(proj(COL_FFW), lb_ref[0:1, :])
    lfb_ref[0] = _log2_forget(proj(COL_FBW), lb_ref[1:2, :])
    if full:
        q_ref, g_ref, hy_ref, mg_ref, par_ref = out_refs[3:]
        zq = proj(COL_Q)
        q_ref[0] = (zq * jax.nn.sigmoid(zq)).astype(BF16)
        zg = proj(COL_G)
        g_ref[0] = (zg * jax.nn.sigmoid(zg)).astype(BF16)
        half = par_ref.shape[1] // 2
        for k in range(3 * HY_W // KW):
            z = proj(COL_HY + k * KW)
            for c in range(KW // LANES):
                par_ref[c] = z[:, c * LANES:(c + 1) * LANES]
                cols = slice(k * KW + c * LANES, k * KW + (c + 1) * LANES)
                for par in range(2):
                    hy_ref[0, par, :, cols] = par_ref[c, pl.ds(par, half, stride=2), :].astype(BF16)
        for k in range(2 * D_MODEL // KW):
            mg_ref[0, :, k * KW:(k + 1) * KW] = jax.nn.sigmoid(proj(COL_MERGE + k * KW)).astype(BF16)


def _input_proj(h, m3, w_in, lb, *, mod_row=None):
    b, l, d = h.shape
    full = mod_row is None
    tm = min(TOKEN_TILE, l)
    nt = l // tm
    row_map = (lambda j, i: (i, 0, 0)) if full else (lambda j, i: (mod_row, 0, 0))
    ncols = IN_COLS if full else COL_Q
    tok = lambda w: pl.BlockSpec((1, tm, w), lambda j, i: (i, j, 0))
    shapes = [(KW, BF16), (KW, F32), (KW, F32)]
    if full:
        shapes += [(KW, BF16), (KW, BF16), None, (2 * D_MODEL, BF16)]
    out_shape = [jax.ShapeDtypeStruct((b, l, s[0]), s[1]) if s else
                 jax.ShapeDtypeStruct((b, 2, l // 2, 3 * HY_W), BF16) for s in shapes]
    out_specs = [tok(s[0]) if s else pl.BlockSpec((1, 2, tm // 2, 3 * HY_W), lambda j, i: (i, 0, j, 0))
                 for s in shapes]
    return pl.pallas_call(
        functools.partial(_proj_kernel, full=full),
        out_shape=out_shape,
        grid=(nt, b),
        in_specs=[
            tok(d),
            pl.BlockSpec((1, N_MOD, d), row_map),
            pl.BlockSpec((d, ncols), lambda j, i: (0, 0), pipeline_mode=pl.Buffered(1)),
            _const_spec(lb.shape),
        ],
        out_specs=out_specs,
        scratch_shapes=[pltpu.VMEM((KW // LANES, tm, LANES), F32)] if full else [],
        compiler_params=_params(2),
        name="proj" if full else "proj_ctx",
    )(h, m3, w_in, lb)


def _split2(x):
    hi = x.astype(BF16)
    lo = (x - hi.astype(F32)).astype(BF16)
    return hi, lo


def _dot_nt(a, b):
    return lax.dot_general(a, b, (((1,), (1,)), ((), ())), preferred_element_type=F32)


def _scan_kernel(q_ref, v_ref, lff_ref, lfb_ref, g_ref, vc_ref, lffc_ref, lfbc_ref, nw_ref,
                 o_ref, oacc_ref, vt_ref, vtc_ref, st_ref):
    c = SCAN_CHUNK
    n_lat = q_ref.shape[1] // c
    n_ctx = vc_ref.shape[1] // c
    row = lax.broadcasted_iota(jnp.int32, (c, c), 0)
    col = lax.broadcasted_iota(jnp.int32, (c, c), 1)
    lower = row >= col
    upper = row <= col
    tri_l = jnp.where(lower, 1.0, 0.0).astype(BF16)
    tri_u = jnp.where(upper, 1.0, 0.0).astype(BF16)

    for j in range(n_lat):
        vt_ref[j] = v_ref[0, j * c:(j + 1) * c, :].astype(F32).T.astype(BF16)
    for j in range(n_ctx):
        vtc_ref[j] = vc_ref[0, j * c:(j + 1) * c, :].astype(F32).T.astype(BF16)

    def chunks(items):
        heads = [slice(hd * HEAD_DIM, (hd + 1) * HEAD_DIM) for hd in range(HEADS)]
        cfg = [(tri_l, lower, c // 2 - 1, c - 1) if d == 0 else (tri_u, upper, c // 2, 0)
               for d, *_ in items]
        split = [_split2(lf) for _, lf, *_ in items]
        cum = [_dot(tri, hi) + _dot(tri, lo)
               for (tri, *_), (hi, lo) in zip(cfg, split)]
        kd, dec, qm, qd, km = [], [], [], [], []
        for (_, lf, _, _, q), (_, _, ref_i, tot_i), b in zip(items, cfg, cum):
            k = 1.0 - jnp.exp2(lf)
            tot = b[tot_i:tot_i + 1, :]
            kd.append((k * jnp.exp2(tot - b)).astype(BF16))
            dec.append(jnp.exp2(tot))
            if q is not None:
                ref = b[ref_i:ref_i + 1, :]
                qf = q.astype(F32)
                qm.append((qf * jnp.exp2(b - ref)).astype(BF16))
                qd.append((qf * jnp.exp2(b)).astype(BF16))
                km.append((k * jnp.exp2(ref - b)).astype(BF16))
            else:
                qm.append(None)
                qd.append(None)
                km.append(None)
        scores = [[_dot_nt(qm[n][:, hs], km[n][:, hs]) for hs in heads] if qm[n] is not None else None
                  for n in range(len(items))]
        grow = [[_dot(items[n][2][hs, :], kd[n][:, hs]) for hs in heads] for n in range(len(items))]
        state = {d: [st_ref[d, hd] for hd in range(HEADS)] for d in {d for d, *_ in items}}
        carry = []
        for n, (d, *_) in enumerate(items):
            carry.append([_dot_nt(qd[n][:, hs], state[d][hd].astype(BF16)) for hd, hs in enumerate(heads)]
                         if qm[n] is not None else None)
            state[d] = [state[d][hd] * dec[n][:, hs] + grow[n][hd] for hd, hs in enumerate(heads)]
        for d, sts in state.items():
            for hd in range(HEADS):
                st_ref[d, hd] = sts[hd]
        outs = []
        for n, (_, _, _, v, q) in enumerate(items):
            if q is None:
                outs.append(None)
                continue
            mask = cfg[n][1]
            o = [_dot(jnp.where(mask, scores[n][hd], 0.0).astype(BF16), v[:, hs]) + carry[n][hd]
                 for hd, hs in enumerate(heads)]
            outs.append(jnp.concatenate(o, axis=-1))
        return outs

    st_ref[...] = jnp.zeros(st_ref.shape, F32)
    chunks([(0, lffc_ref[0, j * c:(j + 1) * c, :], vtc_ref[j], None, None) for j in range(n_ctx)]
           + [(1, lfbc_ref[0, j * c:(j + 1) * c, :], vtc_ref[j], None, None) for j in reversed(range(n_ctx))])

    nw = nw_ref[...]

    def finish(rows, o):
        normed = [_rms(o[:, hd * HEAD_DIM:(hd + 1) * HEAD_DIM]) * nw for hd in range(HEADS)]
        o_ref[0, rows, :] = (jnp.concatenate(normed, axis=-1) * g_ref[0, rows, :].astype(F32)).astype(BF16)

    unroll = SCAN_UNROLL

    def step(j, final):
        idx = [unroll * j + u for u in range(unroll)] + [n_lat - 1 - (unroll * j + u) for u in range(unroll)]
        rows = [pl.ds(pl.multiple_of(i * c, c), c) for i in idx]
        outs = chunks([(0 if n < unroll else 1, (lff_ref if n < unroll else lfb_ref)[0, r, :], vt_ref[i],
                        v_ref[0, r, :], q_ref[0, r, :]) for n, (i, r) in enumerate(zip(idx, rows))])
        for r, o in zip(rows, outs):
            if final:
                finish(r, oacc_ref[r, :] + o)
            else:
                oacc_ref[r, :] = o

    def first_half(j, carry):
        step(j, False)
        return carry

    def second_half(j, carry):
        step(j, True)
        return carry

    n_steps = n_lat // unroll
    lax.fori_loop(0, n_steps // 2, first_half, 0)
    lax.fori_loop(n_steps // 2, n_steps, second_half, 0)


def _hgrn2(q, v, lff, lfb, g, vc, lffc, lfbc, norm_w):
    b, l, _ = q.shape
    lc = vc.shape[1]
    c = SCAN_CHUNK
    seq = lambda n: pl.BlockSpec((1, n, KW), lambda i: (i, 0, 0))
    return pl.pallas_call(
        _scan_kernel,
        out_shape=jax.ShapeDtypeStruct((b, l, KW), BF16),
        grid=(b,),
        in_specs=[seq(l), seq(l), seq(l), seq(l), seq(l), seq(lc), seq(lc), seq(lc),
                  _const_spec(norm_w.shape)],
        out_specs=seq(l),
        scratch_shapes=[pltpu.VMEM((l, KW), F32), pltpu.VMEM((l // c, KW, c), BF16),
                        pltpu.VMEM((lc // c, KW, c), BF16), pltpu.VMEM((2, HEADS, HEAD_DIM, HEAD_DIM), F32)],
        compiler_params=_params(1),
        name="scan",
    )(q, v, lff, lfb, g, vc, lffc, lfbc, norm_w)


def _filter_kernel(z_ref, w1_ref, b1_ref, f1_ref, w2_ref, b2_ref, f2_ref, w3_ref, win_ref, h_ref, mass_ref):
    hp = dict(precision=HIGHEST, preferred_element_type=F32)
    h = jnp.sin(f1_ref[...] * (jnp.dot(z_ref[...], w1_ref[...], **hp) + b1_ref[...]))
    h = jnp.sin(f2_ref[...] * (jnp.dot(h, w2_ref[...], **hp) + b2_ref[...]))
    h = jnp.dot(h, w3_ref[...], **hp)
    win = win_ref[...]
    n_groups = h.shape[1] // HY_W
    h = jnp.concatenate([h[:, k * HY_W:(k + 1) * HY_W] * win for k in range(n_groups)], axis=-1)
    h_ref[...] = h.astype(BF16)

    @pl.when(pl.program_id(0) == 0)
    def _():
        mass_ref[...] = jnp.zeros(mass_ref.shape, F32)

    mass_ref[...] += jnp.sum(jnp.abs(h), axis=0, keepdims=True)


def _hyena_filters(l, w1, b1, fr1, w2, b2, fr2, w3):
    p = jnp.concatenate([jnp.arange(0, l, 2), jnp.arange(1, l, 2)]).astype(F32)
    t = p / (l - 1)
    w = 2.0 * math.pi * p / l
    f = jnp.linspace(1e-4, HYENA_BANDS - 1, HYENA_BANDS, dtype=F32)
    ang = w[:, None] * f[None, :]
    z = jnp.concatenate([t[:, None], jnp.cos(ang), -jnp.sin(ang)], axis=-1)
    max_decay = math.log(HYENA_TARGET) / HYENA_FAST_DECAY
    min_decay = math.log(HYENA_TARGET) / HYENA_SLOW_DECAY
    deltas = jnp.abs(jnp.linspace(min_decay, max_decay, HY_W, dtype=F32))
    window = jnp.exp(-t[:, None] * deltas[None, :]) + HYENA_SHIFT

    pad_c = lambda a, n: jnp.pad(a, ((0, 0), (0, n - a.shape[1])))
    pad_r = lambda a, n: jnp.pad(a, ((0, n - a.shape[0]), (0, 0)))
    z = pad_c(z, FEAT_PAD)
    w1p = pad_c(pad_r(w1, FEAT_PAD), FEAT_PAD)
    w2p = pad_c(pad_r(w2, FEAT_PAD), FEAT_PAD)
    w3p = pad_r(w3, FEAT_PAD)
    vec = lambda a: pad_c(a[None, :], FEAT_PAD)
    n_out = w3.shape[1]
    tl = FILT_TILE
    return pl.pallas_call(
        _filter_kernel,
        out_shape=[jax.ShapeDtypeStruct((l, n_out), BF16), jax.ShapeDtypeStruct((1, n_out), F32)],
        grid=(l // tl,),
        in_specs=[
            pl.BlockSpec((tl, FEAT_PAD), lambda i: (i, 0)),
            _const_spec(w1p.shape), _const_spec((1, FEAT_PAD)), _const_spec((1, FEAT_PAD)),
            _const_spec(w2p.shape), _const_spec((1, FEAT_PAD)), _const_spec((1, FEAT_PAD)),
            _const_spec(w3p.shape),
            pl.BlockSpec((tl, HY_W), lambda i: (i, 0)),
        ],
        out_specs=[pl.BlockSpec((tl, n_out), lambda i: (i, 0)), pl.BlockSpec((1, n_out), lambda i: (0, 0))],
        compiler_params=_params(1),
        name="filt",
    )(z, w1p, vec(b1), vec(fr1), w2p, vec(b2), vec(fr2), w3p, window)


def _dft_kernel(ar_ref, ai_ref, br_ref, bi_ref, cr_ref, ci_ref, dr_ref, di_ref, fwd_ref, inv_ref):
    tf = FREQ_TILE
    br, bi = br_ref[...], bi_ref[...]
    for t1 in range(fwd_ref.shape[1] // LANES):
        ar, ai = ar_ref[:, t1:t1 + 1], ai_ref[:, t1:t1 + 1]
        cols = slice(t1 * LANES, (t1 + 1) * LANES)
        fwd_ref[0:tf, cols] = (ar * br - ai * bi).astype(BF16)
        fwd_ref[tf:2 * tf, cols] = (ar * bi + ai * br).astype(BF16)
    dr, di = dr_ref[...], di_ref[...]
    for f1 in range(tf // LANES):
        cr, ci = cr_ref[0, :, f1:f1 + 1], ci_ref[0, :, f1:f1 + 1]
        inv_ref[0, :, f1 * LANES:(f1 + 1) * LANES] = (cr * dr - ci * di).astype(BF16)
        inv_ref[0, :, tf + f1 * LANES:tf + (f1 + 1) * LANES] = (cr * di + ci * dr).astype(BF16)


def _odd_dft_matrices(l):
    n_ang = 4 * l
    theta = 2.0 * math.pi / n_ang
    tf = FREQ_TILE
    nf = l // tf
    n_hi = l // LANES

    def cis(idx):
        a = (idx % n_ang).astype(F32) * theta
        return jnp.cos(a), jnp.sin(a)

    idx = jnp.arange(l, dtype=jnp.int32)
    odd = 2 * idx + 1
    lane = jnp.arange(LANES, dtype=jnp.int32)
    ar, ai = cis(odd[:, None] * (LANES * jnp.arange(n_hi, dtype=jnp.int32))[None, :])
    br, bi = cis(odd[:, None] * lane[None, :])
    g = jnp.arange(n_hi, dtype=jnp.int32).reshape(nf, 1, tf // LANES)
    cr, ci = cis(idx[None, :, None] * (2 * LANES * g))
    dr, di = cis(idx[:, None] * (2 * lane + 1)[None, :])
    row_tab = lambda w: pl.BlockSpec((tf, w), lambda i: (i, 0))
    return pl.pallas_call(
        _dft_kernel,
        out_shape=[jax.ShapeDtypeStruct((2 * l, l), BF16), jax.ShapeDtypeStruct((nf, l, 2 * tf), BF16)],
        grid=(nf,),
        in_specs=[row_tab(n_hi), row_tab(n_hi), row_tab(LANES), row_tab(LANES),
                  pl.BlockSpec((1, l, tf // LANES), lambda i: (i, 0, 0)),
                  pl.BlockSpec((1, l, tf // LANES), lambda i: (i, 0, 0)),
                  _const_spec((l, LANES)), _const_spec((l, LANES))],
        out_specs=[pl.BlockSpec((2 * tf, l), lambda i: (i, 0)), pl.BlockSpec((1, l, 2 * tf), lambda i: (i, 0, 0))],
        compiler_params=_params(1),
        name="dft",
    )(ar, ai, br, bi, cr, ci, dr, di)


def _radix2_forward(fc, fs, ue, uo, cph, sph):
    ec, es = _dot(fc, ue), _dot(fs, ue)
    oc, os_ = _dot(fc, uo), _dot(fs, uo)
    tc = cph * oc - sph * os_
    ts = cph * os_ + sph * oc
    return ec + tc, es + ts, ec - tc, ts - es


def _kdft_kernel(fwd_ref, te_ref, to_ref, mass_ref, tw_ref, o_ref):
    tf = FREQ_TILE
    l = 2 * te_ref.shape[0]
    fc, fs = fwd_ref[0:tf, :], fwd_ref[tf:2 * tf, :]
    cph, sph, ec, es = (tw_ref[:, k:k + 1] for k in range(4))
    f = pl.program_id(0) * tf + lax.broadcasted_iota(jnp.int32, (tf, 1), 0)
    sgn = jnp.where(f % 2 == 0, 1.0, -1.0)
    for order in range(2):
        fw = slice((2 * order) * HY_W, (2 * order + 1) * HY_W)
        bw = slice((2 * order + 1) * HY_W, (2 * order + 2) * HY_W)
        ac, as_, ahc, ahs = _radix2_forward(fc, fs, te_ref[:, fw], to_ref[:, fw], cph, sph)
        hc, hs, hhc, hhs = _radix2_forward(fc, fs, te_ref[:, bw], to_ref[:, bw], cph, sph)
        scale = (1.0 / l) / (mass_ref[:, fw] + mass_ref[:, bw] + HYENA_L1_EPS)
        cols = slice(order * HY_W, (order + 1) * HY_W)
        o_ref[0, 0, :, cols] = (ac + sgn * (es * hc - ec * hs)) * scale
        o_ref[0, 1, :, cols] = (-as_ + sgn * (ec * hc + es * hs)) * scale
        o_ref[0, 2, :, cols] = (ahc - sgn * (es * hhc + ec * hhs)) * scale
        o_ref[0, 3, :, cols] = (-ahs + sgn * (ec * hhc - es * hhs)) * scale


def _twiddles(l):
    odd = 2 * jnp.arange(l // 2, dtype=jnp.int32) + 1
    theta = 2.0 * math.pi / (4 * l)
    a = odd.astype(F32) * theta
    b = ((odd * (l - 1)) % (4 * l)).astype(F32) * theta
    return jnp.stack([jnp.cos(a), jnp.sin(a), jnp.cos(b), jnp.sin(b)], axis=-1)


def _filter_spectrum(fwd, taps, mass, tw):
    l = taps.shape[0]
    m = l // 2
    tf = FREQ_TILE
    return pl.pallas_call(
        _kdft_kernel,
        out_shape=jax.ShapeDtypeStruct((m // tf, 4, tf, 2 * HY_W), F32),
        grid=(m // tf,),
        in_specs=[pl.BlockSpec((2 * tf, m), lambda i: (i, 0)),
                  pl.BlockSpec((m, taps.shape[1]), lambda i: (0, 0), pipeline_mode=pl.Buffered(1)),
                  pl.BlockSpec((m, taps.shape[1]), lambda i: (1, 0), pipeline_mode=pl.Buffered(1)),
                  _const_spec(mass.shape),
                  pl.BlockSpec((tf, 4), lambda i: (i, 0))],
        out_specs=pl.BlockSpec((1, 4, tf, 2 * HY_W), lambda i: (i, 0, 0, 0)),
        compiler_params=_params(1),
        name="kdft",
    )(fwd, taps, taps, mass, tw)


HYENA_SUB = 128


def _short_conv_parity(x_ref, cw, cb):
    ev = x_ref[0, 0].astype(F32)
    od = x_ref[0, 1].astype(F32)
    m = ev.shape[0]
    t = lax.broadcasted_iota(jnp.int32, (m, 1), 0)
    od_prev = jnp.where(t == 0, 0.0, pltpu.roll(od, 1, 0))
    ev_next = jnp.where(t == m - 1, 0.0, pltpu.roll(ev, m - 1, 0))
    conv_ev = cb + od_prev * cw[0:1, :] + ev * cw[1:2, :] + od * cw[2:3, :]
    conv_od = cb + ev * cw[0:1, :] + od * cw[1:2, :] + ev_next * cw[2:3, :]
    return conv_ev, conv_od


def _hyena_kernel(hv_ref, hx_ref, cw_ref, cb_ref, hb_ref, tw_ref, fwd_ref, inv_ref, kt_ref,
                  o_ref, u_ref, acc_ref, y_ref, nat_ref):
    order = pl.program_id(1)
    i = pl.program_id(2)
    tf = FREQ_TILE
    m, w2 = u_ref.shape
    w = w2 // 2
    last = m // tf - 1

    @pl.when((order == 0) & (i == 0))
    def _():
        ve, vo = _short_conv_parity(hv_ref, cw_ref[0], cb_ref[0])
        u_ref[:, :w] = ve.astype(BF16)
        u_ref[:, w:] = vo.astype(BF16)
        acc_ref[:, :w] = hb_ref[0:1, :] * ve
        acc_ref[:, w:] = hb_ref[0:1, :] * vo

    ue, uo = u_ref[:, :w], u_ref[:, w:]
    for sb in range(tf // HYENA_SUB):
        rc = slice(sb * HYENA_SUB, (sb + 1) * HYENA_SUB)
        rs = slice(tf + sb * HYENA_SUB, tf + (sb + 1) * HYENA_SUB)
        cph, sph = tw_ref[rc, 0:1], tw_ref[rc, 1:2]
        xc, xs, xhc, xhs = _radix2_forward(fwd_ref[rc, :], fwd_ref[rs, :], ue, uo, cph, sph)
        kr, ki, krh, kih = (kt_ref[0, k, rc, :] for k in range(4))
        yc, ys = kr * xc + ki * xs, kr * xs - ki * xc
        yhc, yhs = krh * xhc + kih * xhs, krh * xhs - kih * xhc
        dc, ds = yc - yhc, ys + yhs
        y_ref[rc, :w] = (yc + yhc).astype(BF16)
        y_ref[rs, :w] = (ys - yhs).astype(BF16)
        y_ref[rc, w:] = (dc * cph + ds * sph).astype(BF16)
        y_ref[rs, w:] = (ds * cph - dc * sph).astype(BF16)
    acc_ref[...] += _dot(inv_ref[0], y_ref[...])

    @pl.when((order == 0) & (i == last))
    def _():
        ge, go = _short_conv_parity(hx_ref, cw_ref[1], cb_ref[1])
        ze, zo = ge * acc_ref[:, :w], go * acc_ref[:, w:]
        u_ref[:, :w] = ze.astype(BF16)
        u_ref[:, w:] = zo.astype(BF16)
        acc_ref[:, :w] = hb_ref[1:2, :] * ze
        acc_ref[:, w:] = hb_ref[1:2, :] * zo

    @pl.when((order == 1) & (i == last))
    def _():
        ge, go = _short_conv_parity(hx_ref, cw_ref[2], cb_ref[2])
        ze, zo = ge * acc_ref[:, :w], go * acc_ref[:, w:]
        for k in range(w // LANES):
            cols = slice(k * LANES, (k + 1) * LANES)
            nat_ref[k, pl.ds(0, m, stride=2), :] = ze[:, cols]
            nat_ref[k, pl.ds(1, m, stride=2), :] = zo[:, cols]
            o_ref[0, :, cols] = nat_ref[k].astype(BF16)


def _hyena(hy, conv_w, conv_b, hy_bias, tw, fwd, inv, ktab):
    b, _, m, _ = hy.shape
    tf = FREQ_TILE
    nf = m // tf
    cw = conv_w.reshape(conv_w.shape[0], 3, HY_W).transpose(1, 0, 2)
    cb = conv_b.reshape(3, 1, HY_W)
    return pl.pallas_call(
        _hyena_kernel,
        out_shape=jax.ShapeDtypeStruct((b, 2 * m, HY_W), BF16),
        grid=(b, 2, nf),
        in_specs=[
            pl.BlockSpec((1, 2, m, HY_W), lambda bi, o, i: (bi, 0, 0, 0)),
            pl.BlockSpec((1, 2, m, HY_W), lambda bi, o, i: (bi, 0, 0, 1 + o)),
            _const_spec(cw.shape), _const_spec(cb.shape), _const_spec(hy_bias.shape),
            pl.BlockSpec((tf, 4), lambda bi, o, i: (i, 0)),
            pl.BlockSpec((2 * tf, m), lambda bi, o, i: (i, 0)),
            pl.BlockSpec((1, m, 2 * tf), lambda bi, o, i: (i, 0, 0)),
            pl.BlockSpec((1, 4, tf, HY_W), lambda bi, o, i: (i, 0, 0, o)),
        ],
        out_specs=pl.BlockSpec((1, 2 * m, HY_W), lambda bi, o, i: (bi, 0, 0)),
        scratch_shapes=[pltpu.VMEM((m, 2 * HY_W), BF16), pltpu.VMEM((m, 2 * HY_W), F32),
                        pltpu.VMEM((2 * tf, 2 * HY_W), BF16),
                        pltpu.VMEM((HY_W // LANES, 2 * m, LANES), F32)],
        compiler_params=_params(3),
        name="hyena",
    )(hy, hy, cw, cb, hy_bias, tw, fwd, inv, ktab)


def _grid_pos_embed(n_tokens):
    rows = n_tokens // GRID_W
    quarter = D_MODEL // 4
    omega = 1.0 / (10000.0 ** (jnp.arange(quarter, dtype=F32) / quarter))
    ar = jnp.arange(rows, dtype=F32)[:, None] * omega
    ac = jnp.arange(GRID_W, dtype=F32)[:, None] * omega
    er = jnp.concatenate([jnp.sin(ar), jnp.cos(ar)], axis=-1)
    ec = jnp.concatenate([jnp.sin(ac), jnp.cos(ac)], axis=-1)
    emb = jnp.concatenate([jnp.broadcast_to(er[:, None, :], (rows, GRID_W, D_MODEL // 2)),
                           jnp.broadcast_to(ec[None, :, :], (rows, GRID_W, D_MODEL // 2))], axis=-1)
    return emb.reshape(rows * GRID_W, D_MODEL)


def kernel(x, c, ctx, c_ctx, mod_w, mod_b, ffn_w_gate, ffn_w_up, ffn_w_down, w_in, hgrn_lb_logits,
           hgrn_norm_w, hyena_conv_w, hyena_conv_b, hyena_w1, hyena_b1, hyena_freq1, hyena_w2, hyena_b2,
           hyena_freq2, hyena_w3, hyena_bias, w_proj_a, w_proj_b, w_out, final_norm_w):
    assert mod_w.shape[0] == 1, "single-layer configuration"
    batch, n_lat, d = x.shape

    c_all = jnp.concatenate([c, c_ctx[None, :]], axis=0)
    c_all = jnp.pad(c_all, ((0, -c_all.shape[0] % 8), (0, 0)))
    m3 = _modulation(c_all, mod_w[0], mod_b[0][None, :]).reshape(c_all.shape[0], N_MOD, d)

    lb = jnp.cumsum(jax.nn.softmax(hgrn_lb_logits.astype(F32), axis=0), axis=0)[0]
    wg, wu, wd = _to_bf16(ffn_w_gate[0]), _to_bf16(ffn_w_up[0]), _to_bf16(ffn_w_down[0])
    w_in_b = _to_bf16(w_in[0])

    h1 = _half_ffn(x, m3, wg, wu, wd, 0, mod_base=0, pos=_grid_pos_embed(n_lat))
    hc1 = _half_ffn(ctx, m3, wg, wu, wd, 0, mod_base=0, mod_row=batch)

    vc, lffc, lfbc = _input_proj(hc1, m3, w_in_b, lb, mod_row=batch)
    v, lff, lfb, q, g, hy, sg = _input_proj(h1, m3, w_in_b, lb)
    o_a = _hgrn2(q, v, lff, lfb, g, vc, lffc, lfbc, hgrn_norm_w[0][None, :])

    taps, mass = _hyena_filters(n_lat, hyena_w1[0], hyena_b1[0], hyena_freq1[0], hyena_w2[0], hyena_b2[0],
                                hyena_freq2[0], hyena_w3[0])
    tw = _twiddles(n_lat)
    fwd, inv = _odd_dft_matrices(n_lat // 2)
    ktab = _filter_spectrum(fwd, taps, mass, tw)
    o_b = _hyena(hy, hyena_conv_w[0], hyena_conv_b[0], hyena_bias[0], tw, fwd, inv, ktab)

    mixers = (o_a, o_b, sg, _to_bf16(w_proj_a[0]), _to_bf16(w_proj_b[0]), _to_bf16(w_out[0]))
    return _half_ffn(h1, m3, wg, wu, wd, 1, mod_base=6, mixers=mixers, final_norm_w=final_norm_w[None, :])
```

```python
import functools
import math

import jax
import jax.numpy as jnp
from jax import lax
from jax.experimental import pallas as pl
from jax.experimental.pallas import tpu as pltpu

F32 = jnp.float32
BF16 = jnp.bfloat16
HIGHEST = lax.Precision.HIGHEST

D_MODEL = 1024
GRID_W = 64
HEADS = 4
HEAD_DIM = 128
KW = HEADS * HEAD_DIM
HY_W = 512
D_FF = 2816
N_MOD = 9
RMS_EPS = 1e-6
HYENA_EMB = 33
HYENA_BANDS = (HYENA_EMB - 1) // 2
HYENA_FFN = 64
HYENA_FAST_DECAY = 0.3
HYENA_SLOW_DECAY = 1.5
HYENA_TARGET = 1e-2
HYENA_SHIFT = 0.05
HYENA_L1_EPS = 1e-6

COL_V = 0
COL_FFW = COL_V + KW
COL_FBW = COL_FFW + KW
COL_Q = COL_FBW + KW
COL_G = COL_Q + KW
COL_HY = COL_G + KW
COL_MERGE = COL_HY + 3 * HY_W
IN_COLS = COL_MERGE + 2 * D_MODEL

V7X_VMEM_BYTES = 64 * 1024 * 1024
VMEM_LIMIT = V7X_VMEM_BYTES - 8 * 1024 * 1024

TOKEN_TILE = 512
PROJ_TILE = 1024
FFN_TILE = 1024
FF_CHUNK = 256
SCAN_CHUNK = 64
SCAN_UNROLL = 2
FREQ_TILE = 256
TIME_SPLIT = 4
FILT_TILE = 256
FEAT_PAD = 128
LANES = 128


def _const_spec(shape):
    nd = len(shape)
    return pl.BlockSpec(shape, lambda *_: (0,) * nd, pipeline_mode=pl.Buffered(1))


def _params(n_grid):
    return pltpu.CompilerParams(dimension_semantics=("arbitrary",) * n_grid, vmem_limit_bytes=VMEM_LIMIT)


def _rms(x):
    return x * lax.rsqrt(jnp.mean(x * x, axis=-1, keepdims=True) + RMS_EPS)


def _norm_mod(h, shift, scale):
    return _rms(h) * (1.0 + scale) + shift


def _dot(a, b):
    return jnp.dot(a, b, preferred_element_type=F32)


CAST_BLOCK_BYTES = 4 * 1024 * 1024


def _cast_kernel(x_ref, o_ref):
    o_ref[...] = x_ref[...].astype(BF16)


def _to_bf16(w):
    w2 = w.reshape(-1, w.shape[-1])
    r, c = w2.shape
    tr = next(r // k for k in range(1, r + 1)
              if r % k == 0 and (r // k) % 8 == 0 and (r // k) * c * 4 <= CAST_BLOCK_BYTES)
    out = pl.pallas_call(
        _cast_kernel,
        out_shape=jax.ShapeDtypeStruct((r, c), BF16),
        grid=(r // tr,),
        in_specs=[pl.BlockSpec((tr, c), lambda i: (i, 0))],
        out_specs=pl.BlockSpec((tr, c), lambda i: (i, 0)),
        compiler_params=_params(1),
        name="cast",
    )(w2)
    return out.reshape(w.shape)


def _mod_kernel(c_ref, w_ref, b_ref, o_ref):
    c = c_ref[...]
    a = c * jax.nn.sigmoid(c)
    o_ref[...] = jnp.dot(a, w_ref[...], precision=HIGHEST, preferred_element_type=F32) + b_ref[...]


def _modulation(c_all, mod_w, mod_b):
    rows = c_all.shape[0]
    tn = D_MODEL
    return pl.pallas_call(
        _mod_kernel,
        out_shape=jax.ShapeDtypeStruct((rows, N_MOD * D_MODEL), F32),
        grid=(N_MOD,),
        in_specs=[
            pl.BlockSpec((rows, D_MODEL), lambda j: (0, 0)),
            pl.BlockSpec((D_MODEL, tn), lambda j: (0, j)),
            pl.BlockSpec((1, tn), lambda j: (0, j)),
        ],
        out_specs=pl.BlockSpec((rows, tn), lambda j: (0, j)),
        compiler_params=_params(1),
        name="mod",
    )(c_all, mod_w, mod_b)


def _ffn_kernel(*refs, mod_base, add_pos, mixers, final_norm):
    refs = list(refs)
    h_ref = refs.pop(0)
    pos_ref = refs.pop(0) if add_pos else None
    m_ref = refs.pop(0)
    if mixers:
        oa_ref, ob_ref, sg_ref, wpa_ref, wpb_ref, wo_ref = refs[:6]
        refs = refs[6:]
    wg_ref, wu_ref, wd_ref = refs[:3]
    refs = refs[3:]
    fnw_ref = refs.pop(0) if final_norm else None
    o_ref = refs.pop(0)

    h = h_ref[0]
    if add_pos:
        h = h + pos_ref[...]
    if mixers:
        d = h.shape[1]
        ya = _dot(oa_ref[0], wpa_ref[...])
        yb = _dot(ob_ref[0], wpb_ref[...])
        y = sg_ref[0, :, :d].astype(F32) * ya + sg_ref[0, :, d:].astype(F32) * yb
        h = h + m_ref[0, 5:6, :] * _dot(y.astype(BF16), wo_ref[...])
    shift = m_ref[0, mod_base:mod_base + 1, :]
    scale = m_ref[0, mod_base + 1:mod_base + 2, :]
    gate = m_ref[0, mod_base + 2:mod_base + 3, :]
    nb = _norm_mod(h, shift, scale).astype(BF16)
    acc = jnp.zeros(h.shape, F32)
    for c in range(D_FF // FF_CHUNK):
        sl = slice(c * FF_CHUNK, (c + 1) * FF_CHUNK)
        g = _dot(nb, wg_ref[0, :, sl])
        u = _dot(nb, wu_ref[0, :, sl])
        a = (g * jax.nn.sigmoid(g) * u).astype(BF16)
        acc = acc + _dot(a, wd_ref[0, sl, :])
    out = h + 0.5 * gate * acc
    if final_norm:
        out = _rms(out) * fnw_ref[...]
    o_ref[0] = out


def _half_ffn(h, m3, wg, wu, wd, half, *, mod_base, mod_row=None, pos=None, mixers=None, final_norm_w=None):
    b, l, d = h.shape
    tm = min(TOKEN_TILE if mixers is not None else FFN_TILE, l)
    nt = l // tm
    row_map = (lambda j, i: (i, 0, 0)) if mod_row is None else (lambda j, i: (mod_row, 0, 0))
    tok = lambda w: pl.BlockSpec((1, tm, w), lambda j, i: (i, j, 0))
    in_specs = [tok(d)]
    args = [h]
    if pos is not None:
        in_specs.append(pl.BlockSpec((tm, d), lambda j, i: (j, 0)))
        args.append(pos)
    in_specs.append(pl.BlockSpec((1, N_MOD, d), row_map))
    args.append(m3)
    if mixers is not None:
        in_specs += [tok(a.shape[2]) for a in mixers[:3]] + [_const_spec(w.shape) for w in mixers[3:]]
        args += list(mixers)
    in_specs += [pl.BlockSpec((1,) + w.shape[1:], lambda j, i: (half, 0, 0), pipeline_mode=pl.Buffered(1))
                 for w in (wg, wu, wd)]
    args += [wg, wu, wd]
    if final_norm_w is not None:
        in_specs.append(_const_spec(final_norm_w.shape))
        args.append(final_norm_w)
    kern = functools.partial(_ffn_kernel, mod_base=mod_base, add_pos=pos is not None,
                             mixers=mixers is not None, final_norm=final_norm_w is not None)
    return pl.pallas_call(
        kern,
        out_shape=jax.ShapeDtypeStruct((b, l, d), F32),
        grid=(nt, b),
        in_specs=in_specs,
        out_specs=tok(d),
        compiler_params=_params(2),
        name="ffn_mix" if mixers is not None else "ffn",
    )(*args)


def _log2_forget(z, lb):
    return jnp.log2(lb + (1.0 - lb) * jax.nn.sigmoid(z))


def _proj_kernel(h_ref, m_ref, w_ref, lb_ref, *out_refs, full):
    h = h_ref[0]
    nb = _norm_mod(h, m_ref[0, 3:4, :], m_ref[0, 4:5, :]).astype(BF16)

    def proj(c0):
        return _dot(nb, w_ref[:, c0:c0 + KW])

    v_ref, lff_ref, lfb_ref = out_refs[:3]
    v_ref[0] = proj(COL_V).astype(BF16)
    lff_ref[0] = _log2_forget(proj(COL_FFW), lb_ref[0:1, :])
    lfb_ref[0] = _log2_forget(proj(COL_FBW), lb_ref[1:2, :])
    if full:
        q_ref, g_ref, hy_ref, mg_ref, par_ref = out_refs[3:]
        zq = proj(COL_Q)
        q_ref[0] = (zq * jax.nn.sigmoid(zq)).astype(BF16)
        zg = proj(COL_G)
        g_ref[0] = (zg * jax.nn.sigmoid(zg)).astype(BF16)
        part = par_ref.shape[1] // TIME_SPLIT
        for k in range(3 * HY_W // KW):
            z = proj(COL_HY + k * KW)
            for c in range(KW // LANES):
                par_ref[c] = z[:, c * LANES:(c + 1) * LANES]
                cols = slice(k * KW + c * LANES, k * KW + (c + 1) * LANES)
                for r in range(TIME_SPLIT):
                    hy_ref[0, r, :, cols] = par_ref[c, pl.ds(r, part, stride=TIME_SPLIT), :].astype(BF16)
        for k in range(2 * D_MODEL // KW):
            mg_ref[0, :, k * KW:(k + 1) * KW] = jax.nn.sigmoid(proj(COL_MERGE + k * KW)).astype(BF16)


def _input_proj(h, m3, w_in, lb, *, mod_row=None):
    b, l, d = h.shape
    full = mod_row is None
    tm = min(PROJ_TILE, l)
    nt = l // tm
    row_map = (lambda j, i: (i, 0, 0)) if full else (lambda j, i: (mod_row, 0, 0))
    ncols = IN_COLS if full else COL_Q
    tok = lambda w: pl.BlockSpec((1, tm, w), lambda j, i: (i, j, 0))
    shapes = [(KW, BF16), (KW, F32), (KW, F32)]
    if full:
        shapes += [(KW, BF16), (KW, BF16), None, (2 * D_MODEL, BF16)]
    out_shape = [jax.ShapeDtypeStruct((b, l, s[0]), s[1]) if s else
                 jax.ShapeDtypeStruct((b, TIME_SPLIT, l // TIME_SPLIT, 3 * HY_W), BF16) for s in shapes]
    out_specs = [tok(s[0]) if s else
                 pl.BlockSpec((1, TIME_SPLIT, tm // TIME_SPLIT, 3 * HY_W), lambda j, i: (i, 0, j, 0))
                 for s in shapes]
    return pl.pallas_call(
        functools.partial(_proj_kernel, full=full),
        out_shape=out_shape,
        grid=(nt, b),
        in_specs=[
            tok(d),
            pl.BlockSpec((1, N_MOD, d), row_map),
            pl.BlockSpec((d, ncols), lambda j, i: (0, 0), pipeline_mode=pl.Buffered(1)),
            _const_spec(lb.shape),
        ],
        out_specs=out_specs,
        scratch_shapes=[pltpu.VMEM((KW // LANES, tm, LANES), F32)] if full else [],
        compiler_params=_params(2),
        name="proj" if full else "proj_ctx",
    )(h, m3, w_in, lb)


def _split2(x):
    hi = x.astype(BF16)
    lo = (x - hi.astype(F32)).astype(BF16)
    return hi, lo


def _dot_nt(a, b):
    return lax.dot_general(a, b, (((1,), (1,)), ((), ())), preferred_element_type=F32)


def _scan_kernel(q_ref, v_ref, lff_ref, lfb_ref, g_ref, vc_ref, lffc_ref, lfbc_ref, nw_ref,
                 o_ref, oacc_ref, vt_ref, vtc_ref, st_ref):
    c = SCAN_CHUNK
    n_lat = q_ref.shape[1] // c
    n_ctx = vc_ref.shape[1] // c
    row = lax.broadcasted_iota(jnp.int32, (c, c), 0)
    col = lax.broadcasted_iota(jnp.int32, (c, c), 1)
    lower = row >= col
    upper = row <= col
    tri_l = jnp.where(lower, 1.0, 0.0).astype(BF16)
    tri_u = jnp.where(upper, 1.0, 0.0).astype(BF16)

    for j in range(n_lat):
        vt_ref[j] = v_ref[0, j * c:(j + 1) * c, :].astype(F32).T.astype(BF16)
    for j in range(n_ctx):
        vtc_ref[j] = vc_ref[0, j * c:(j + 1) * c, :].astype(F32).T.astype(BF16)

    heads = [slice(hd * HEAD_DIM, (hd + 1) * HEAD_DIM) for hd in range(HEADS)]

    def cumulate(direction, lf):
        hi, lo = _split2(lf)
        tri = tri_l if direction == 0 else tri_u
        return _dot(tri, hi) + _dot(tri, lo)

    def prepare(direction, lf, q, b=None):
        ref_i, tot_i = (c // 2 - 1, c - 1) if direction == 0 else (c // 2, 0)
        b = cumulate(direction, lf) if b is None else b
        k = 1.0 - jnp.exp2(lf)
        tot = b[tot_i:tot_i + 1, :]
        kd = (k * jnp.exp2(tot - b)).astype(BF16)
        dec = jnp.exp2(tot)
        if q is None:
            return kd, dec, None, None, None
        ref = b[ref_i:ref_i + 1, :]
        qf = q.astype(F32)
        qm = (qf * jnp.exp2(b - ref)).astype(BF16)
        qd = (qf * jnp.exp2(b)).astype(BF16)
        km = (k * jnp.exp2(ref - b)).astype(BF16)
        return kd, dec, qm, qd, km

    def advance(items):
        with_q = [p[2] is not None for _, p, _, _ in items]
        scores = [[_dot_nt(p[2][:, hs], p[4][:, hs]) for hs in heads] if wq else None
                  for (_, p, _, _), wq in zip(items, with_q)]
        grow = [[_dot(vt[hs, :], p[0][:, hs]) for hs in heads] for _, p, vt, _ in items]
        state = {d: [st_ref[d, hd] for hd in range(HEADS)] for d in {d for d, *_ in items}}
        carry = []
        for n, (d, p, _, _) in enumerate(items):
            carry.append([_dot_nt(p[3][:, hs], state[d][hd].astype(BF16)) for hd, hs in enumerate(heads)]
                         if with_q[n] else None)
            state[d] = [state[d][hd] * p[1][:, hs] + grow[n][hd] for hd, hs in enumerate(heads)]
        for d, sts in state.items():
            for hd in range(HEADS):
                st_ref[d, hd] = sts[hd]
        outs = []
        for n, (d, _, _, v) in enumerate(items):
            if not with_q[n]:
                outs.append(None)
                continue
            mask = lower if d == 0 else upper
            o = [_dot(jnp.where(mask, scores[n][hd], 0.0).astype(BF16), v[:, hs]) + carry[n][hd]
                 for hd, hs in enumerate(heads)]
            outs.append(jnp.concatenate(o, axis=-1))
        return outs

    st_ref[...] = jnp.zeros(st_ref.shape, F32)
    advance([(0, prepare(0, lffc_ref[0, j * c:(j + 1) * c, :], None), vtc_ref[j], None) for j in range(n_ctx)]
            + [(1, prepare(1, lfbc_ref[0, j * c:(j + 1) * c, :], None), vtc_ref[j], None)
               for j in reversed(range(n_ctx))])

    nw = nw_ref[...]

    def finish(rows, o):
        normed = [_rms(o[:, hs]) * nw for hs in heads]
        o_ref[0, rows, :] = (jnp.concatenate(normed, axis=-1) * g_ref[0, rows, :].astype(F32)).astype(BF16)

    unroll = SCAN_UNROLL
    n_steps = n_lat // unroll

    def step_chunks(j):
        idx = [unroll * j + u for u in range(unroll)] + [n_lat - 1 - (unroll * j + u) for u in range(unroll)]
        start = (lambda i: i * c) if isinstance(j, int) else (lambda i: pl.multiple_of(i * c, c))
        return [(0 if n < unroll else 1, i, pl.ds(start(i), c)) for n, i in enumerate(idx)]

    def log_forget(d, r):
        return (lff_ref if d == 0 else lfb_ref)[0, r, :]

    def step(j, final):
        todo = step_chunks(j)
        cums = [cumulate(d, log_forget(d, r)) for d, _, r in todo]
        ops = [prepare(d, log_forget(d, r), q_ref[0, r, :], b) for (d, _, r), b in zip(todo, cums)]
        outs = advance([(d, p, vt_ref[i], v_ref[0, r, :]) for (d, i, r), p in zip(todo, ops)])
        for (_, _, r), o in zip(todo, outs):
            if final:
                finish(r, oacc_ref[r, :] + o)
            else:
                oacc_ref[r, :] = o

    def first_half(j, carry):
        step(j, False)
        return carry

    def second_half(j, carry):
        step(j, True)
        return carry

    lax.fori_loop(0, n_steps // 2, first_half, 0)
    lax.fori_loop(n_steps // 2, n_steps, second_half, 0)


def _hgrn2(q, v, lff, lfb, g, vc, lffc, lfbc, norm_w):
    b, l, _ = q.shape
    lc = vc.shape[1]
    c = SCAN_CHUNK
    seq = lambda n: pl.BlockSpec((1, n, KW), lambda i: (i, 0, 0))
    return pl.pallas_call(
        _scan_kernel,
        out_shape=jax.ShapeDtypeStruct((b, l, KW), BF16),
        grid=(b,),
        in_specs=[seq(l), seq(l), seq(l), seq(l), seq(l), seq(lc), seq(lc), seq(lc),
                  _const_spec(norm_w.shape)],
        out_specs=seq(l),
        scratch_shapes=[pltpu.VMEM((l, KW), F32), pltpu.VMEM((l // c, KW, c), BF16),
                        pltpu.VMEM((lc // c, KW, c), BF16), pltpu.VMEM((2, HEADS, HEAD_DIM, HEAD_DIM), F32)],
        compiler_params=_params(1),
        name="scan",
    )(q, v, lff, lfb, g, vc, lffc, lfbc, norm_w)


def _filter_kernel(z_ref, w1_ref, b1_ref, f1_ref, w2_ref, b2_ref, f2_ref, w3_ref, win_ref, h_ref, mass_ref):
    hp = dict(precision=HIGHEST, preferred_element_type=F32)
    h = jnp.sin(f1_ref[...] * (jnp.dot(z_ref[...], w1_ref[...], **hp) + b1_ref[...]))
    h = jnp.sin(f2_ref[...] * (jnp.dot(h, w2_ref[...], **hp) + b2_ref[...]))
    h = jnp.dot(h, w3_ref[...], **hp)
    win = win_ref[...]
    n_groups = h.shape[1] // HY_W
    h = jnp.concatenate([h[:, k * HY_W:(k + 1) * HY_W] * win for k in range(n_groups)], axis=-1)
    h_ref[...] = h.astype(BF16)

    @pl.when(pl.program_id(0) == 0)
    def _():
        mass_ref[...] = jnp.zeros(mass_ref.shape, F32)

    mass_ref[...] += jnp.sum(jnp.abs(h), axis=0, keepdims=True)


def _hyena_filters(l, w1, b1, fr1, w2, b2, fr2, w3):
    p = jnp.concatenate([jnp.arange(r, l, TIME_SPLIT) for r in range(TIME_SPLIT)]).astype(F32)
    t = p / (l - 1)
    w = 2.0 * math.pi * p / l
    f = jnp.linspace(1e-4, HYENA_BANDS - 1, HYENA_BANDS, dtype=F32)
    ang = w[:, None] * f[None, :]
    z = jnp.concatenate([t[:, None], jnp.cos(ang), -jnp.sin(ang)], axis=-1)
    max_decay = math.log(HYENA_TARGET) / HYENA_FAST_DECAY
    min_decay = math.log(HYENA_TARGET) / HYENA_SLOW_DECAY
    deltas = jnp.abs(jnp.linspace(min_decay, max_decay, HY_W, dtype=F32))
    window = jnp.exp(-t[:, None] * deltas[None, :]) + HYENA_SHIFT

    pad_c = lambda a, n: jnp.pad(a, ((0, 0), (0, n - a.shape[1])))
    pad_r = lambda a, n: jnp.pad(a, ((0, n - a.shape[0]), (0, 0)))
    z = pad_c(z, FEAT_PAD)
    w1p = pad_c(pad_r(w1, FEAT_PAD), FEAT_PAD)
    w2p = pad_c(pad_r(w2, FEAT_PAD), FEAT_PAD)
    w3p = pad_r(w3, FEAT_PAD)
    vec = lambda a: pad_c(a[None, :], FEAT_PAD)
    n_out = w3.shape[1]
    tl = FILT_TILE
    return pl.pallas_call(
        _filter_kernel,
        out_shape=[jax.ShapeDtypeStruct((l, n_out), BF16), jax.ShapeDtypeStruct((1, n_out), F32)],
        grid=(l // tl,),
        in_specs=[
            pl.BlockSpec((tl, FEAT_PAD), lambda i: (i, 0)),
            _const_spec(w1p.shape), _const_spec((1, FEAT_PAD)), _const_spec((1, FEAT_PAD)),
            _const_spec(w2p.shape), _const_spec((1, FEAT_PAD)), _const_spec((1, FEAT_PAD)),
            _const_spec(w3p.shape),
            pl.BlockSpec((tl, HY_W), lambda i: (i, 0)),
        ],
        out_specs=[pl.BlockSpec((tl, n_out), lambda i: (i, 0)), pl.BlockSpec((1, n_out), lambda i: (0, 0))],
        compiler_params=_params(1),
        name="filt",
    )(z, w1p, vec(b1), vec(fr1), w2p, vec(b2), vec(fr2), w3p, window)


def _dft_kernel(ar_ref, ai_ref, br_ref, bi_ref, cr_ref, ci_ref, dr_ref, di_ref, fwd_ref, inv_ref):
    tf = FREQ_TILE
    br, bi = br_ref[...], bi_ref[...]
    for t1 in range(fwd_ref.shape[1] // LANES):
        ar, ai = ar_ref[:, t1:t1 + 1], ai_ref[:, t1:t1 + 1]
        cols = slice(t1 * LANES, (t1 + 1) * LANES)
        fwd_ref[0:tf, cols] = (ar * br - ai * bi).astype(BF16)
        fwd_ref[tf:2 * tf, cols] = (ar * bi + ai * br).astype(BF16)
    dr, di = dr_ref[...], di_ref[...]
    for f1 in range(tf // LANES):
        cr, ci = cr_ref[0, :, f1:f1 + 1], ci_ref[0, :, f1:f1 + 1]
        inv_ref[0, :, f1 * LANES:(f1 + 1) * LANES] = (cr * dr - ci * di).astype(BF16)
        inv_ref[0, :, tf + f1 * LANES:tf + (f1 + 1) * LANES] = (cr * di + ci * dr).astype(BF16)


def _odd_dft_matrices(l):
    n_ang = 4 * l
    theta = 2.0 * math.pi / n_ang
    tf = FREQ_TILE
    nf = l // tf
    n_hi = l // LANES

    def cis(idx):
        a = (idx % n_ang).astype(F32) * theta
        return jnp.cos(a), jnp.sin(a)

    idx = jnp.arange(l, dtype=jnp.int32)
    odd = 2 * idx + 1
    lane = jnp.arange(LANES, dtype=jnp.int32)
    ar, ai = cis(odd[:, None] * (LANES * jnp.arange(n_hi, dtype=jnp.int32))[None, :])
    br, bi = cis(odd[:, None] * lane[None, :])
    g = jnp.arange(n_hi, dtype=jnp.int32).reshape(nf, 1, tf // LANES)
    cr, ci = cis(idx[None, :, None] * (2 * LANES * g))
    dr, di = cis(idx[:, None] * (2 * lane + 1)[None, :])
    row_tab = lambda w: pl.BlockSpec((tf, w), lambda i: (i, 0))
    return pl.pallas_call(
        _dft_kernel,
        out_shape=[jax.ShapeDtypeStruct((2 * l, l), BF16), jax.ShapeDtypeStruct((nf, l, 2 * tf), BF16)],
        grid=(nf,),
        in_specs=[row_tab(n_hi), row_tab(n_hi), row_tab(LANES), row_tab(LANES),
                  pl.BlockSpec((1, l, tf // LANES), lambda i: (i, 0, 0)),
                  pl.BlockSpec((1, l, tf // LANES), lambda i: (i, 0, 0)),
                  _const_spec((l, LANES)), _const_spec((l, LANES))],
        out_specs=[pl.BlockSpec((2 * tf, l), lambda i: (i, 0)), pl.BlockSpec((1, l, 2 * tf), lambda i: (i, 0, 0))],
        compiler_params=_params(1),
        name="dft",
    )(ar, ai, br, bi, cr, ci, dr, di)


def _butterfly(ac, as_, bc, bs, cph, sph):
    tc = cph * bc - sph * bs
    ts = cph * bs + sph * bc
    return ac + tc, as_ + ts, ac - tc, ts - as_


def _inv_butterfly(yc, ys, yhc, yhs, cph, sph):
    dc, ds = yc - yhc, ys + yhs
    return yc + yhc, ys - yhs, dc * cph + ds * sph, ds * cph - dc * sph


def _split4_forward(fc, fs, xs, tw):
    c1, s1, c1g, s1g, c2, s2 = (tw[:, k:k + 1] for k in range(6))
    p = [(_dot(fc, x), _dot(fs, x)) for x in xs]
    ev = _butterfly(*p[0], *p[2], c2, s2)
    od = _butterfly(*p[1], *p[3], c2, s2)
    return _butterfly(*ev[:2], *od[:2], c1, s1) + _butterfly(*ev[2:], *od[2:], c1g, s1g)


def _split4_inverse(y, tw):
    c1, s1, c1g, s1g, c2, s2 = (tw[:, k:k + 1] for k in range(6))
    at_f = _inv_butterfly(*y[0:4], c1, s1)
    at_g = _inv_butterfly(*y[4:8], c1g, s1g)
    r0c, r0s, r2c, r2s = _inv_butterfly(*at_f[:2], *at_g[:2], c2, s2)
    r1c, r1s, r3c, r3s = _inv_butterfly(*at_f[2:], *at_g[2:], c2, s2)
    return [(r0c, r0s), (r1c, r1s), (r2c, r2s), (r3c, r3s)]


def _kdft_kernel(fwd_ref, t0_ref, t1_ref, t2_ref, t3_ref, mass_ref, tw_ref, o_ref):
    tf = FREQ_TILE
    taps = (t0_ref, t1_ref, t2_ref, t3_ref)
    l = TIME_SPLIT * t0_ref.shape[0]
    fc, fs = fwd_ref[0:tf, :], fwd_ref[tf:2 * tf, :]
    tw = tw_ref[...]
    f = pl.program_id(0) * tf + lax.broadcasted_iota(jnp.int32, (tf, 1), 0)
    sgn_f = jnp.where(f % 2 == 0, 1.0, -1.0)
    signs = (sgn_f, -sgn_f, -sgn_f, sgn_f)
    for order in range(2):
        fw = slice((2 * order) * HY_W, (2 * order + 1) * HY_W)
        bw = slice((2 * order + 1) * HY_W, (2 * order + 2) * HY_W)
        a = _split4_forward(fc, fs, [t[:, fw] for t in taps], tw)
        h = _split4_forward(fc, fs, [t[:, bw] for t in taps], tw)
        scale = (1.0 / l) / (mass_ref[:, fw] + mass_ref[:, bw] + HYENA_L1_EPS)
        cols = slice(order * HY_W, (order + 1) * HY_W)
        for k in range(4):
            ac, as_, hc, hs = a[2 * k], a[2 * k + 1], h[2 * k], h[2 * k + 1]
            ec, es = tw[:, 6 + 2 * k:7 + 2 * k], tw[:, 7 + 2 * k:8 + 2 * k]
            o_ref[0, 2 * k, :, cols] = (ac + signs[k] * (es * hc - ec * hs)) * scale
            o_ref[0, 2 * k + 1, :, cols] = (-as_ + signs[k] * (ec * hc + es * hs)) * scale


def _twiddles(l):
    n_ang = 4 * l
    theta = 2.0 * math.pi / n_ang
    f = jnp.arange(l // 4, dtype=jnp.int32)
    g = l // 2 - 1 - f
    odd = lambda x: 2 * x + 1
    angles = [odd(f), odd(g), 2 * odd(f)] + [odd(x) * (l - 1) for x in (f, l - 1 - f, g, l - 1 - g)]
    cols = []
    for a in angles:
        r = (a % n_ang).astype(F32) * theta
        cols += [jnp.cos(r), jnp.sin(r)]
    cols += [jnp.zeros_like(cols[0])] * 2
    return jnp.stack(cols, axis=-1)


def _filter_spectrum(fwd, taps, mass, tw):
    l = taps.shape[0]
    m = l // TIME_SPLIT
    tf = FREQ_TILE
    tap_block = lambda r: pl.BlockSpec((m, taps.shape[1]), lambda i: (r, 0), pipeline_mode=pl.Buffered(1))
    return pl.pallas_call(
        _kdft_kernel,
        out_shape=jax.ShapeDtypeStruct((m // tf, 8, tf, 2 * HY_W), F32),
        grid=(m // tf,),
        in_specs=[pl.BlockSpec((2 * tf, m), lambda i: (i, 0))] + [tap_block(r) for r in range(TIME_SPLIT)]
        + [_const_spec(mass.shape), pl.BlockSpec((tf, tw.shape[1]), lambda i: (i, 0))],
        out_specs=pl.BlockSpec((1, 8, tf, 2 * HY_W), lambda i: (i, 0, 0, 0)),
        compiler_params=_params(1),
        name="kdft",
    )(fwd, taps, taps, taps, taps, mass, tw)


HYENA_SUB = 128


def _short_conv_split(x_ref, cw, cb):
    x = [x_ref[0, r].astype(F32) for r in range(TIME_SPLIT)]
    m = x[0].shape[0]
    t = lax.broadcasted_iota(jnp.int32, (m, 1), 0)
    before = jnp.where(t == 0, 0.0, pltpu.roll(x[-1], 1, 0))
    after = jnp.where(t == m - 1, 0.0, pltpu.roll(x[0], m - 1, 0))
    prev = [before] + x[:-1]
    nxt = x[1:] + [after]
    return [cb + prev[r] * cw[0:1, :] + x[r] * cw[1:2, :] + nxt[r] * cw[2:3, :] for r in range(TIME_SPLIT)]


def _hyena_kernel(hv_ref, hx_ref, cw_ref, cb_ref, hb_ref, tw_ref, fwd_ref, inv_ref, kt_ref,
                  o_ref, u_ref, acc_ref, y_ref, nat_ref):
    order = pl.program_id(1)
    i = pl.program_id(2)
    tf = FREQ_TILE
    m, wide = u_ref.shape
    w = wide // TIME_SPLIT
    last = m // tf - 1
    lanes = [slice(r * w, (r + 1) * w) for r in range(TIME_SPLIT)]

    def restart(parts, bias):
        for r, part in enumerate(parts):
            u_ref[:, lanes[r]] = part.astype(BF16)
            acc_ref[:, lanes[r]] = bias * part

    @pl.when((order == 0) & (i == 0))
    def _():
        restart(_short_conv_split(hv_ref, cw_ref[0], cb_ref[0]), hb_ref[0:1, :])

    xs = [u_ref[:, ln] for ln in lanes]
    for sb in range(tf // HYENA_SUB):
        rc = slice(sb * HYENA_SUB, (sb + 1) * HYENA_SUB)
        rs = slice(tf + sb * HYENA_SUB, tf + (sb + 1) * HYENA_SUB)
        tw = tw_ref[rc, :]
        x = _split4_forward(fwd_ref[rc, :], fwd_ref[rs, :], xs, tw)
        y = []
        for k in range(4):
            kr, ki = kt_ref[0, 2 * k, rc, :], kt_ref[0, 2 * k + 1, rc, :]
            y += [kr * x[2 * k] + ki * x[2 * k + 1], kr * x[2 * k + 1] - ki * x[2 * k]]
        for r, (yc, ys) in enumerate(_split4_inverse(y, tw)):
            y_ref[rc, lanes[r]] = yc.astype(BF16)
            y_ref[rs, lanes[r]] = ys.astype(BF16)
    acc_ref[...] += _dot(inv_ref[0], y_ref[...])

    @pl.when((order == 0) & (i == last))
    def _():
        gates = _short_conv_split(hx_ref, cw_ref[1], cb_ref[1])
        restart([g * acc_ref[:, ln] for g, ln in zip(gates, lanes)], hb_ref[1:2, :])

    @pl.when((order == 1) & (i == last))
    def _():
        gates = _short_conv_split(hx_ref, cw_ref[2], cb_ref[2])
        z = [g * acc_ref[:, ln] for g, ln in zip(gates, lanes)]
        for k in range(w // LANES):
            cols = slice(k * LANES, (k + 1) * LANES)
            for r in range(TIME_SPLIT):
                nat_ref[k, pl.ds(r, m, stride=TIME_SPLIT), :] = z[r][:, cols]
            o_ref[0, :, cols] = nat_ref[k].astype(BF16)


def _hyena(hy, conv_w, conv_b, hy_bias, tw, fwd, inv, ktab):
    b, _, m, _ = hy.shape
    tf = FREQ_TILE
    nf = m // tf
    cw = conv_w.reshape(conv_w.shape[0], 3, HY_W).transpose(1, 0, 2)
    cb = conv_b.reshape(3, 1, HY_W)
    return pl.pallas_call(
        _hyena_kernel,
        out_shape=jax.ShapeDtypeStruct((b, TIME_SPLIT * m, HY_W), BF16),
        grid=(b, 2, nf),
        in_specs=[
            pl.BlockSpec((1, TIME_SPLIT, m, HY_W), lambda bi, o, i: (bi, 0, 0, 0)),
            pl.BlockSpec((1, TIME_SPLIT, m, HY_W), lambda bi, o, i: (bi, 0, 0, 1 + o)),
            _const_spec(cw.shape), _const_spec(cb.shape), _const_spec(hy_bias.shape),
            pl.BlockSpec((tf, tw.shape[1]), lambda bi, o, i: (i, 0)),
            pl.BlockSpec((2 * tf, m), lambda bi, o, i: (i, 0)),
            pl.BlockSpec((1, m, 2 * tf), lambda bi, o, i: (i, 0, 0)),
            pl.BlockSpec((1, 8, tf, HY_W), lambda bi, o, i: (i, 0, 0, o)),
        ],
        out_specs=pl.BlockSpec((1, TIME_SPLIT * m, HY_W), lambda bi, o, i: (bi, 0, 0)),
        scratch_shapes=[pltpu.VMEM((m, TIME_SPLIT * HY_W), BF16), pltpu.VMEM((m, TIME_SPLIT * HY_W), F32),
                        pltpu.VMEM((2 * tf, TIME_SPLIT * HY_W), BF16),
                        pltpu.VMEM((HY_W // LANES, TIME_SPLIT * m, LANES), F32)],
        compiler_params=_params(3),
        name="hyena",
    )(hy, hy, cw, cb, hy_bias, tw, fwd, inv, ktab)


def _grid_pos_embed(n_tokens):
    rows = n_tokens // GRID_W
    quarter = D_MODEL // 4
    omega = 1.0 / (10000.0 ** (jnp.arange(quarter, dtype=F32) / quarter))
    ar = jnp.arange(rows, dtype=F32)[:, None] * omega
    ac = jnp.arange(GRID_W, dtype=F32)[:, None] * omega
    er = jnp.concatenate([jnp.sin(ar), jnp.cos(ar)], axis=-1)
    ec = jnp.concatenate([jnp.sin(ac), jnp.cos(ac)], axis=-1)
    emb = jnp.concatenate([jnp.broadcast_to(er[:, None, :], (rows, GRID_W, D_MODEL // 2)),
                           jnp.broadcast_to(ec[None, :, :], (rows, GRID_W, D_MODEL // 2))], axis=-1)
    return emb.reshape(rows * GRID_W, D_MODEL)


def kernel(x, c, ctx, c_ctx, mod_w, mod_b, ffn_w_gate, ffn_w_up, ffn_w_down, w_in, hgrn_lb_logits,
           hgrn_norm_w, hyena_conv_w, hyena_conv_b, hyena_w1, hyena_b1, hyena_freq1, hyena_w2, hyena_b2,
           hyena_freq2, hyena_w3, hyena_bias, w_proj_a, w_proj_b, w_out, final_norm_w):
    assert mod_w.shape[0] == 1, "single-layer configuration"
    batch, n_lat, d = x.shape

    c_all = jnp.concatenate([c, c_ctx[None, :]], axis=0)
    c_all = jnp.pad(c_all, ((0, -c_all.shape[0] % 8), (0, 0)))
    m3 = _modulation(c_all, mod_w[0], mod_b[0][None, :]).reshape(c_all.shape[0], N_MOD, d)

    lb = jnp.cumsum(jax.nn.softmax(hgrn_lb_logits.astype(F32), axis=0), axis=0)[0]
    wg, wu, wd = _to_bf16(ffn_w_gate[0]), _to_bf16(ffn_w_up[0]), _to_bf16(ffn_w_down[0])
    w_in_b = _to_bf16(w_in[0])

    h1 = _half_ffn(x, m3, wg, wu, wd, 0, mod_base=0, pos=_grid_pos_embed(n_lat))
    hc1 = _half_ffn(ctx, m3, wg, wu, wd, 0, mod_base=0, mod_row=batch)

    vc, lffc, lfbc = _input_proj(hc1, m3, w_in_b, lb, mod_row=batch)
    v, lff, lfb, q, g, hy, sg = _input_proj(h1, m3, w_in_b, lb)
    o_a = _hgrn2(q, v, lff, lfb, g, vc, lffc, lfbc, hgrn_norm_w[0][None, :])

    taps, mass = _hyena_filters(n_lat, hyena_w1[0], hyena_b1[0], hyena_freq1[0], hyena_w2[0], hyena_b2[0],
                                hyena_freq2[0], hyena_w3[0])
    tw = _twiddles(n_lat)
    fwd, inv = _odd_dft_matrices(n_lat // TIME_SPLIT)
    ktab = _filter_spectrum(fwd, taps, mass, tw)
    o_b = _hyena(hy, hyena_conv_w[0], hyena_conv_b[0], hyena_bias[0], tw, fwd, inv, ktab)

    mixers = (o_a, o_b, sg, _to_bf16(w_proj_a[0]), _to_bf16(w_proj_b[0]), _to_bf16(w_out[0]))
    return _half_ffn(h1, m3, wg, wu, wd, 1, mod_base=6, mixers=mixers, final_norm_w=final_norm_w[None, :])
```

```python
import functools
import math

import jax
import jax.numpy as jnp
from jax import lax
from jax.experimental import pallas as pl
from jax.experimental.pallas import tpu as pltpu

F32 = jnp.float32
BF16 = jnp.bfloat16
HIGHEST = lax.Precision.HIGHEST

D_MODEL = 1024
GRID_W = 64
HEADS = 4
HEAD_DIM = 128
KW = HEADS * HEAD_DIM
HY_W = 512
D_FF = 2816
N_MOD = 9
RMS_EPS = 1e-6
HYENA_EMB = 33
HYENA_BANDS = (HYENA_EMB - 1) // 2
HYENA_FFN = 64
HYENA_FAST_DECAY = 0.3
HYENA_SLOW_DECAY = 1.5
HYENA_TARGET = 1e-2
HYENA_SHIFT = 0.05
HYENA_L1_EPS = 1e-6

COL_V = 0
COL_FFW = COL_V + KW
COL_FBW = COL_FFW + KW
COL_Q = COL_FBW + KW
COL_G = COL_Q + KW
COL_HY = COL_G + KW
COL_MERGE = COL_HY + 3 * HY_W
IN_COLS = COL_MERGE + 2 * D_MODEL

V7X_VMEM_BYTES = 64 * 1024 * 1024
VMEM_LIMIT = V7X_VMEM_BYTES - 8 * 1024 * 1024

TOKEN_TILE = 512
PROJ_TILE = 1024
FFN_TILE = 1024
FF_CHUNK = 256
SCAN_CHUNK = 64
SCAN_UNROLL = 2
FREQ_TILE = 256
TIME_SPLIT = 4
FILT_TILE = 256
FEAT_PAD = 128
LANES = 128


def _const_spec(shape):
    nd = len(shape)
    return pl.BlockSpec(shape, lambda *_: (0,) * nd, pipeline_mode=pl.Buffered(1))


def _params(n_grid):
    return pltpu.CompilerParams(dimension_semantics=("arbitrary",) * n_grid, vmem_limit_bytes=VMEM_LIMIT)


def _rms(x):
    return x * lax.rsqrt(jnp.mean(x * x, axis=-1, keepdims=True) + RMS_EPS)


def _norm_mod(h, shift, scale):
    return _rms(h) * (1.0 + scale) + shift


def _dot(a, b):
    return jnp.dot(a, b, preferred_element_type=F32)


CAST_BLOCK_BYTES = 4 * 1024 * 1024


def _cast_kernel(x_ref, o_ref):
    o_ref[...] = x_ref[...].astype(BF16)


def _to_bf16(w):
    w2 = w.reshape(-1, w.shape[-1])
    r, c = w2.shape
    tr = next(r // k for k in range(1, r + 1)
              if r % k == 0 and (r // k) % 8 == 0 and (r // k) * c * 4 <= CAST_BLOCK_BYTES)
    out = pl.pallas_call(
        _cast_kernel,
        out_shape=jax.ShapeDtypeStruct((r, c), BF16),
        grid=(r // tr,),
        in_specs=[pl.BlockSpec((tr, c), lambda i: (i, 0))],
        out_specs=pl.BlockSpec((tr, c), lambda i: (i, 0)),
        compiler_params=_params(1),
        name="cast",
    )(w2)
    return out.reshape(w.shape)


def _mod_kernel(c_ref, w_ref, b_ref, o_ref):
    c = c_ref[...]
    a = c * jax.nn.sigmoid(c)
    o_ref[...] = _dot(a.astype(BF16), w_ref[...].astype(BF16)) + b_ref[...]


def _modulation(c_all, mod_w, mod_b):
    rows = c_all.shape[0]
    tn = D_MODEL
    return pl.pallas_call(
        _mod_kernel,
        out_shape=jax.ShapeDtypeStruct((rows, N_MOD * D_MODEL), F32),
        grid=(N_MOD,),
        in_specs=[
            pl.BlockSpec((rows, D_MODEL), lambda j: (0, 0)),
            pl.BlockSpec((D_MODEL, tn), lambda j: (0, j)),
            pl.BlockSpec((1, tn), lambda j: (0, j)),
        ],
        out_specs=pl.BlockSpec((rows, tn), lambda j: (0, j)),
        compiler_params=_params(1),
        name="mod",
    )(c_all, mod_w, mod_b)


def _ffn_kernel(*refs, mod_base, add_pos, mixers, final_norm):
    refs = list(refs)
    h_ref = refs.pop(0)
    pos_ref = refs.pop(0) if add_pos else None
    m_ref = refs.pop(0)
    if mixers:
        oa_ref, ob_ref, sg_ref, wpa_ref, wpb_ref, wo_ref = refs[:6]
        refs = refs[6:]
    wg_ref, wu_ref, wd_ref = refs[:3]
    refs = refs[3:]
    fnw_ref = refs.pop(0) if final_norm else None
    o_ref = refs.pop(0)

    h = h_ref[0]
    if add_pos:
        h = h + pos_ref[...]
    if mixers:
        d = h.shape[1]
        ya = _dot(oa_ref[0], wpa_ref[...])
        yb = _dot(ob_ref[0], wpb_ref[...])
        y = sg_ref[0, :, :d].astype(F32) * ya + sg_ref[0, :, d:].astype(F32) * yb
        h = h + m_ref[0, 5:6, :] * _dot(y.astype(BF16), wo_ref[...])
    shift = m_ref[0, mod_base:mod_base + 1, :]
    scale = m_ref[0, mod_base + 1:mod_base + 2, :]
    gate = m_ref[0, mod_base + 2:mod_base + 3, :]
    nb = _norm_mod(h, shift, scale).astype(BF16)
    acc = jnp.zeros(h.shape, F32)
    for c in range(D_FF // FF_CHUNK):
        sl = slice(c * FF_CHUNK, (c + 1) * FF_CHUNK)
        g = _dot(nb, wg_ref[0, :, sl])
        u = _dot(nb, wu_ref[0, :, sl])
        a = (g * jax.nn.sigmoid(g) * u).astype(BF16)
        acc = acc + _dot(a, wd_ref[0, sl, :])
    out = h + 0.5 * gate * acc
    if final_norm:
        out = _rms(out) * fnw_ref[...]
    o_ref[0] = out


def _half_ffn(h, m3, wg, wu, wd, half, *, mod_base, mod_row=None, pos=None, mixers=None, final_norm_w=None):
    b, l, d = h.shape
    tm = min(TOKEN_TILE if mixers is not None else FFN_TILE, l)
    nt = l // tm
    row_map = (lambda j, i: (i, 0, 0)) if mod_row is None else (lambda j, i: (mod_row, 0, 0))
    tok = lambda w: pl.BlockSpec((1, tm, w), lambda j, i: (i, j, 0))
    in_specs = [tok(d)]
    args = [h]
    if pos is not None:
        in_specs.append(pl.BlockSpec((tm, d), lambda j, i: (j, 0)))
        args.append(pos)
    in_specs.append(pl.BlockSpec((1, N_MOD, d), row_map))
    args.append(m3)
    if mixers is not None:
        in_specs += [tok(a.shape[2]) for a in mixers[:3]] + [_const_spec(w.shape) for w in mixers[3:]]
        args += list(mixers)
    in_specs += [pl.BlockSpec((1,) + w.shape[1:], lambda j, i: (half, 0, 0), pipeline_mode=pl.Buffered(1))
                 for w in (wg, wu, wd)]
    args += [wg, wu, wd]
    if final_norm_w is not None:
        in_specs.append(_const_spec(final_norm_w.shape))
        args.append(final_norm_w)
    kern = functools.partial(_ffn_kernel, mod_base=mod_base, add_pos=pos is not None,
                             mixers=mixers is not None, final_norm=final_norm_w is not None)
    return pl.pallas_call(
        kern,
        out_shape=jax.ShapeDtypeStruct((b, l, d), F32),
        grid=(nt, b),
        in_specs=in_specs,
        out_specs=tok(d),
        compiler_params=_params(2),
        name="ffn_mix" if mixers is not None else "ffn",
    )(*args)


def _log2_forget(z, lb):
    return jnp.log2(lb + (1.0 - lb) * jax.nn.sigmoid(z))


def _proj_kernel(h_ref, m_ref, w_ref, lb_ref, *out_refs, full):
    h = h_ref[0]
    nb = _norm_mod(h, m_ref[0, 3:4, :], m_ref[0, 4:5, :]).astype(BF16)

    def proj(c0):
        return _dot(nb, w_ref[:, c0:c0 + KW])

    v_ref, lff_ref, lfb_ref = out_refs[:3]
    v_ref[0] = proj(COL_V).astype(BF16)
    lff_ref[0] = _log2_forget(proj(COL_FFW), lb_ref[0:1, :])
    lfb_ref[0] = _log2_forget(proj(COL_FBW), lb_ref[1:2, :])
    if full:
        q_ref, g_ref, hy_ref, mg_ref, par_ref = out_refs[3:]
        zq = proj(COL_Q)
        q_ref[0] = (zq * jax.nn.sigmoid(zq)).astype(BF16)
        zg = proj(COL_G)
        g_ref[0] = (zg * jax.nn.sigmoid(zg)).astype(BF16)
        part = par_ref.shape[1] // TIME_SPLIT
        for k in range(3 * HY_W // KW):
            z = proj(COL_HY + k * KW)
            for c in range(KW // LANES):
                par_ref[c] = z[:, c * LANES:(c + 1) * LANES]
                cols = slice(k * KW + c * LANES, k * KW + (c + 1) * LANES)
                for r in range(TIME_SPLIT):
                    hy_ref[0, r, :, cols] = par_ref[c, pl.ds(r, part, stride=TIME_SPLIT), :].astype(BF16)
        for k in range(2 * D_MODEL // KW):
            mg_ref[0, :, k * KW:(k + 1) * KW] = jax.nn.sigmoid(proj(COL_MERGE + k * KW)).astype(BF16)


def _input_proj(h, m3, w_in, lb, *, mod_row=None):
    b, l, d = h.shape
    full = mod_row is None
    tm = min(PROJ_TILE, l)
    nt = l // tm
    row_map = (lambda j, i: (i, 0, 0)) if full else (lambda j, i: (mod_row, 0, 0))
    ncols = IN_COLS if full else COL_Q
    tok = lambda w: pl.BlockSpec((1, tm, w), lambda j, i: (i, j, 0))
    shapes = [(KW, BF16), (KW, F32), (KW, F32)]
    if full:
        shapes += [(KW, BF16), (KW, BF16), None, (2 * D_MODEL, BF16)]
    out_shape = [jax.ShapeDtypeStruct((b, l, s[0]), s[1]) if s else
                 jax.ShapeDtypeStruct((b, TIME_SPLIT, l // TIME_SPLIT, 3 * HY_W), BF16) for s in shapes]
    out_specs = [tok(s[0]) if s else
                 pl.BlockSpec((1, TIME_SPLIT, tm // TIME_SPLIT, 3 * HY_W), lambda j, i: (i, 0, j, 0))
                 for s in shapes]
    return pl.pallas_call(
        functools.partial(_proj_kernel, full=full),
        out_shape=out_shape,
        grid=(nt, b),
        in_specs=[
            tok(d),
            pl.BlockSpec((1, N_MOD, d), row_map),
            pl.BlockSpec((d, ncols), lambda j, i: (0, 0), pipeline_mode=pl.Buffered(1)),
            _const_spec(lb.shape),
        ],
        out_specs=out_specs,
        scratch_shapes=[pltpu.VMEM((KW // LANES, tm, LANES), F32)] if full else [],
        compiler_params=_params(2),
        name="proj" if full else "proj_ctx",
    )(h, m3, w_in, lb)


def _split2(x):
    hi = x.astype(BF16)
    lo = (x - hi.astype(F32)).astype(BF16)
    return hi, lo


def _dot_nt(a, b):
    return lax.dot_general(a, b, (((1,), (1,)), ((), ())), preferred_element_type=F32)


def _scan_kernel(q_ref, v_ref, lff_ref, lfb_ref, g_ref, vc_ref, lffc_ref, lfbc_ref, nw_ref,
                 o_ref, oacc_ref, vt_ref, vtc_ref, st_ref):
    c = SCAN_CHUNK
    n_lat = q_ref.shape[1] // c
    n_ctx = vc_ref.shape[1] // c
    row = lax.broadcasted_iota(jnp.int32, (c, c), 0)
    col = lax.broadcasted_iota(jnp.int32, (c, c), 1)
    lower = row >= col
    upper = row <= col
    tri_l = jnp.where(lower, 1.0, 0.0).astype(BF16)
    tri_u = jnp.where(upper, 1.0, 0.0).astype(BF16)

    for j in range(n_lat):
        vt_ref[j] = v_ref[0, j * c:(j + 1) * c, :].astype(F32).T.astype(BF16)
    for j in range(n_ctx):
        vtc_ref[j] = vc_ref[0, j * c:(j + 1) * c, :].astype(F32).T.astype(BF16)

    heads = [slice(hd * HEAD_DIM, (hd + 1) * HEAD_DIM) for hd in range(HEADS)]

    def cumulate(direction, lf):
        hi, lo = _split2(lf)
        tri = tri_l if direction == 0 else tri_u
        return _dot(tri, hi) + _dot(tri, lo)

    def prepare(direction, lf, q, b=None):
        ref_i, tot_i = (c // 2 - 1, c - 1) if direction == 0 else (c // 2, 0)
        b = cumulate(direction, lf) if b is None else b
        k = 1.0 - jnp.exp2(lf)
        tot = b[tot_i:tot_i + 1, :]
        kd = (k * jnp.exp2(tot - b)).astype(BF16)
        dec = jnp.exp2(tot)
        if q is None:
            return kd, dec, None, None, None
        ref = b[ref_i:ref_i + 1, :]
        qf = q.astype(F32)
        qm = (qf * jnp.exp2(b - ref)).astype(BF16)
        qd = (qf * jnp.exp2(b)).astype(BF16)
        km = (k * jnp.exp2(ref - b)).astype(BF16)
        return kd, dec, qm, qd, km

    def advance(items):
        with_q = [p[2] is not None for _, p, _, _ in items]
        scores = [[_dot_nt(p[2][:, hs], p[4][:, hs]) for hs in heads] if wq else None
                  for (_, p, _, _), wq in zip(items, with_q)]
        grow = [[_dot(vt[hs, :], p[0][:, hs]) for hs in heads] for _, p, vt, _ in items]
        state = {d: [st_ref[d, hd] for hd in range(HEADS)] for d in {d for d, *_ in items}}
        carry = []
        for n, (d, p, _, _) in enumerate(items):
            carry.append([_dot_nt(p[3][:, hs], state[d][hd].astype(BF16)) for hd, hs in enumerate(heads)]
                         if with_q[n] else None)
            state[d] = [state[d][hd] * p[1][:, hs] + grow[n][hd] for hd, hs in enumerate(heads)]
        for d, sts in state.items():
            for hd in range(HEADS):
                st_ref[d, hd] = sts[hd]
        outs = []
        for n, (d, _, _, v) in enumerate(items):
            if not with_q[n]:
                outs.append(None)
                continue
            mask = lower if d == 0 else upper
            o = [_dot(jnp.where(mask, scores[n][hd], 0.0).astype(BF16), v[:, hs]) + carry[n][hd]
                 for hd, hs in enumerate(heads)]
            outs.append(jnp.concatenate(o, axis=-1))
        return outs

    st_ref[...] = jnp.zeros(st_ref.shape, F32)
    advance([(0, prepare(0, lffc_ref[0, j * c:(j + 1) * c, :], None), vtc_ref[j], None) for j in range(n_ctx)]
            + [(1, prepare(1, lfbc_ref[0, j * c:(j + 1) * c, :], None), vtc_ref[j], None)
               for j in reversed(range(n_ctx))])

    nw = nw_ref[...]

    def finish(rows, o):
        normed = [_rms(o[:, hs]) * nw for hs in heads]
        o_ref[0, rows, :] = (jnp.concatenate(normed, axis=-1) * g_ref[0, rows, :].astype(F32)).astype(BF16)

    unroll = SCAN_UNROLL
    n_steps = n_lat // unroll

    def step_chunks(j):
        idx = [unroll * j + u for u in range(unroll)] + [n_lat - 1 - (unroll * j + u) for u in range(unroll)]
        start = (lambda i: i * c) if isinstance(j, int) else (lambda i: pl.multiple_of(i * c, c))
        return [(0 if n < unroll else 1, i, pl.ds(start(i), c)) for n, i in enumerate(idx)]

    def log_forget(d, r):
        return (lff_ref if d == 0 else lfb_ref)[0, r, :]

    def step(j, final):
        todo = step_chunks(j)
        cums = [cumulate(d, log_forget(d, r)) for d, _, r in todo]
        ops = [prepare(d, log_forget(d, r), q_ref[0, r, :], b) for (d, _, r), b in zip(todo, cums)]
        outs = advance([(d, p, vt_ref[i], v_ref[0, r, :]) for (d, i, r), p in zip(todo, ops)])
        for (_, _, r), o in zip(todo, outs):
            if final:
                finish(r, oacc_ref[r, :] + o)
            else:
                oacc_ref[r, :] = o

    def first_half(j, carry):
        step(j, False)
        return carry

    def second_half(j, carry):
        step(j, True)
        return carry

    lax.fori_loop(0, n_steps // 2, first_half, 0)
    lax.fori_loop(n_steps // 2, n_steps, second_half, 0)


def _hgrn2(q, v, lff, lfb, g, vc, lffc, lfbc, norm_w):
    b, l, _ = q.shape
    lc = vc.shape[1]
    c = SCAN_CHUNK
    seq = lambda n: pl.BlockSpec((1, n, KW), lambda i: (i, 0, 0))
    return pl.pallas_call(
        _scan_kernel,
        out_shape=jax.ShapeDtypeStruct((b, l, KW), BF16),
        grid=(b,),
        in_specs=[seq(l), seq(l), seq(l), seq(l), seq(l), seq(lc), seq(lc), seq(lc),
                  _const_spec(norm_w.shape)],
        out_specs=seq(l),
        scratch_shapes=[pltpu.VMEM((l, KW), F32), pltpu.VMEM((l // c, KW, c), BF16),
                        pltpu.VMEM((lc // c, KW, c), BF16), pltpu.VMEM((2, HEADS, HEAD_DIM, HEAD_DIM), F32)],
        compiler_params=_params(1),
        name="scan",
    )(q, v, lff, lfb, g, vc, lffc, lfbc, norm_w)


def _filter_kernel(z_ref, w1_ref, b1_ref, f1_ref, w2_ref, b2_ref, f2_ref, w3_ref, win_ref, h_ref, mass_ref):
    hp = dict(precision=HIGHEST, preferred_element_type=F32)
    h = jnp.sin(f1_ref[...] * (jnp.dot(z_ref[...], w1_ref[...], **hp) + b1_ref[...]))
    h = jnp.sin(f2_ref[...] * (jnp.dot(h, w2_ref[...], **hp) + b2_ref[...]))
    (h_hi, h_lo), (w_hi, w_lo) = _split2(h), _split2(w3_ref[...])
    h = _dot(h_hi, w_hi) + _dot(h_hi, w_lo) + _dot(h_lo, w_hi)
    win = win_ref[...]
    n_groups = h.shape[1] // HY_W
    h = jnp.concatenate([h[:, k * HY_W:(k + 1) * HY_W] * win for k in range(n_groups)], axis=-1)
    h_ref[...] = h.astype(BF16)

    @pl.when(pl.program_id(0) == 0)
    def _():
        mass_ref[...] = jnp.zeros(mass_ref.shape, F32)

    mass_ref[...] += jnp.sum(jnp.abs(h), axis=0, keepdims=True)


def _hyena_filters(l, w1, b1, fr1, w2, b2, fr2, w3):
    p = jnp.concatenate([jnp.arange(r, l, TIME_SPLIT) for r in range(TIME_SPLIT)]).astype(F32)
    t = p / (l - 1)
    w = 2.0 * math.pi * p / l
    f = jnp.linspace(1e-4, HYENA_BANDS - 1, HYENA_BANDS, dtype=F32)
    ang = w[:, None] * f[None, :]
    z = jnp.concatenate([t[:, None], jnp.cos(ang), -jnp.sin(ang)], axis=-1)
    max_decay = math.log(HYENA_TARGET) / HYENA_FAST_DECAY
    min_decay = math.log(HYENA_TARGET) / HYENA_SLOW_DECAY
    deltas = jnp.abs(jnp.linspace(min_decay, max_decay, HY_W, dtype=F32))
    window = jnp.exp(-t[:, None] * deltas[None, :]) + HYENA_SHIFT

    pad_c = lambda a, n: jnp.pad(a, ((0, 0), (0, n - a.shape[1])))
    pad_r = lambda a, n: jnp.pad(a, ((0, n - a.shape[0]), (0, 0)))
    z = pad_c(z, FEAT_PAD)
    w1p = pad_c(pad_r(w1, FEAT_PAD), FEAT_PAD)
    w2p = pad_c(pad_r(w2, FEAT_PAD), FEAT_PAD)
    w3p = pad_r(w3, FEAT_PAD)
    vec = lambda a: pad_c(a[None, :], FEAT_PAD)
    n_out = w3.shape[1]
    tl = FILT_TILE
    return pl.pallas_call(
        _filter_kernel,
        out_shape=[jax.ShapeDtypeStruct((l, n_out), BF16), jax.ShapeDtypeStruct((1, n_out), F32)],
        grid=(l // tl,),
        in_specs=[
            pl.BlockSpec((tl, FEAT_PAD), lambda i: (i, 0)),
            _const_spec(w1p.shape), _const_spec((1, FEAT_PAD)), _const_spec((1, FEAT_PAD)),
            _const_spec(w2p.shape), _const_spec((1, FEAT_PAD)), _const_spec((1, FEAT_PAD)),
            _const_spec(w3p.shape),
            pl.BlockSpec((tl, HY_W), lambda i: (i, 0)),
        ],
        out_specs=[pl.BlockSpec((tl, n_out), lambda i: (i, 0)), pl.BlockSpec((1, n_out), lambda i: (0, 0))],
        compiler_params=_params(1),
        name="filt",
    )(z, w1p, vec(b1), vec(fr1), w2p, vec(b2), vec(fr2), w3p, window)


def _dft_kernel(ar_ref, ai_ref, br_ref, bi_ref, cr_ref, ci_ref, dr_ref, di_ref, fwd_ref, inv_ref):
    tf = FREQ_TILE
    br, bi = br_ref[...], bi_ref[...]
    for t1 in range(fwd_ref.shape[1] // LANES):
        ar, ai = ar_ref[:, t1:t1 + 1], ai_ref[:, t1:t1 + 1]
        cols = slice(t1 * LANES, (t1 + 1) * LANES)
        fwd_ref[0:tf, cols] = (ar * br - ai * bi).astype(BF16)
        fwd_ref[tf:2 * tf, cols] = (ar * bi + ai * br).astype(BF16)
    dr, di = dr_ref[...], di_ref[...]
    for f1 in range(tf // LANES):
        cr, ci = cr_ref[0, :, f1:f1 + 1], ci_ref[0, :, f1:f1 + 1]
        inv_ref[0, :, f1 * LANES:(f1 + 1) * LANES] = (cr * dr - ci * di).astype(BF16)
        inv_ref[0, :, tf + f1 * LANES:tf + (f1 + 1) * LANES] = (cr * di + ci * dr).astype(BF16)


def _odd_dft_matrices(l):
    n_ang = 4 * l
    theta = 2.0 * math.pi / n_ang
    tf = FREQ_TILE
    nf = l // tf
    n_hi = l // LANES

    def cis(idx):
        a = (idx % n_ang).astype(F32) * theta
        return jnp.cos(a), jnp.sin(a)

    idx = jnp.arange(l, dtype=jnp.int32)
    odd = 2 * idx + 1
    lane = jnp.arange(LANES, dtype=jnp.int32)
    ar, ai = cis(odd[:, None] * (LANES * jnp.arange(n_hi, dtype=jnp.int32))[None, :])
    br, bi = cis(odd[:, None] * lane[None, :])
    g = jnp.arange(n_hi, dtype=jnp.int32).reshape(nf, 1, tf // LANES)
    cr, ci = cis(idx[None, :, None] * (2 * LANES * g))
    dr, di = cis(idx[:, None] * (2 * lane + 1)[None, :])
    row_tab = lambda w: pl.BlockSpec((tf, w), lambda i: (i, 0))
    return pl.pallas_call(
        _dft_kernel,
        out_shape=[jax.ShapeDtypeStruct((2 * l, l), BF16), jax.ShapeDtypeStruct((nf, l, 2 * tf), BF16)],
        grid=(nf,),
        in_specs=[row_tab(n_hi), row_tab(n_hi), row_tab(LANES), row_tab(LANES),
                  pl.BlockSpec((1, l, tf // LANES), lambda i: (i, 0, 0)),
                  pl.BlockSpec((1, l, tf // LANES), lambda i: (i, 0, 0)),
                  _const_spec((l, LANES)), _const_spec((l, LANES))],
        out_specs=[pl.BlockSpec((2 * tf, l), lambda i: (i, 0)), pl.BlockSpec((1, l, 2 * tf), lambda i: (i, 0, 0))],
        compiler_params=_params(1),
        name="dft",
    )(ar, ai, br, bi, cr, ci, dr, di)


def _butterfly(ac, as_, bc, bs, cph, sph):
    tc = cph * bc - sph * bs
    ts = cph * bs + sph * bc
    return ac + tc, as_ + ts, ac - tc, ts - as_


def _inv_butterfly(yc, ys, yhc, yhs, cph, sph):
    dc, ds = yc - yhc, ys + yhs
    return yc + yhc, ys - yhs, dc * cph + ds * sph, ds * cph - dc * sph


def _split4_forward(fc, fs, xs, tw):
    c1, s1, c1g, s1g, c2, s2 = (tw[:, k:k + 1] for k in range(6))
    p = [(_dot(fc, x), _dot(fs, x)) for x in xs]
    ev = _butterfly(*p[0], *p[2], c2, s2)
    od = _butterfly(*p[1], *p[3], c2, s2)
    return _butterfly(*ev[:2], *od[:2], c1, s1) + _butterfly(*ev[2:], *od[2:], c1g, s1g)


def _split4_inverse(y, tw):
    c1, s1, c1g, s1g, c2, s2 = (tw[:, k:k + 1] for k in range(6))
    at_f = _inv_butterfly(*y[0:4], c1, s1)
    at_g = _inv_butterfly(*y[4:8], c1g, s1g)
    r0c, r0s, r2c, r2s = _inv_butterfly(*at_f[:2], *at_g[:2], c2, s2)
    r1c, r1s, r3c, r3s = _inv_butterfly(*at_f[2:], *at_g[2:], c2, s2)
    return [(r0c, r0s), (r1c, r1s), (r2c, r2s), (r3c, r3s)]


def _kdft_kernel(fwd_ref, t0_ref, t1_ref, t2_ref, t3_ref, mass_ref, tw_ref, o_ref):
    tf = FREQ_TILE
    taps = (t0_ref, t1_ref, t2_ref, t3_ref)
    l = TIME_SPLIT * t0_ref.shape[0]
    fc, fs = fwd_ref[0:tf, :], fwd_ref[tf:2 * tf, :]
    tw = tw_ref[...]
    f = pl.program_id(0) * tf + lax.broadcasted_iota(jnp.int32, (tf, 1), 0)
    sgn_f = jnp.where(f % 2 == 0, 1.0, -1.0)
    signs = (sgn_f, -sgn_f, -sgn_f, sgn_f)
    for order in range(2):
        fw = slice((2 * order) * HY_W, (2 * order + 1) * HY_W)
        bw = slice((2 * order + 1) * HY_W, (2 * order + 2) * HY_W)
        a = _split4_forward(fc, fs, [t[:, fw] for t in taps], tw)
        h = _split4_forward(fc, fs, [t[:, bw] for t in taps], tw)
        scale = (1.0 / l) / (mass_ref[:, fw] + mass_ref[:, bw] + HYENA_L1_EPS)
        cols = slice(order * HY_W, (order + 1) * HY_W)
        for k in range(4):
            ac, as_, hc, hs = a[2 * k], a[2 * k + 1], h[2 * k], h[2 * k + 1]
            ec, es = tw[:, 6 + 2 * k:7 + 2 * k], tw[:, 7 + 2 * k:8 + 2 * k]
            o_ref[0, 2 * k, :, cols] = (ac + signs[k] * (es * hc - ec * hs)) * scale
            o_ref[0, 2 * k + 1, :, cols] = (-as_ + signs[k] * (ec * hc + es * hs)) * scale


def _twiddles(l):
    n_ang = 4 * l
    theta = 2.0 * math.pi / n_ang
    f = jnp.arange(l // 4, dtype=jnp.int32)
    g = l // 2 - 1 - f
    odd = lambda x: 2 * x + 1
    angles = [odd(f), odd(g), 2 * odd(f)] + [odd(x) * (l - 1) for x in (f, l - 1 - f, g, l - 1 - g)]
    cols = []
    for a in angles:
        r = (a % n_ang).astype(F32) * theta
        cols += [jnp.cos(r), jnp.sin(r)]
    cols += [jnp.zeros_like(cols[0])] * 2
    return jnp.stack(cols, axis=-1)


def _filter_spectrum(fwd, taps, mass, tw):
    l = taps.shape[0]
    m = l // TIME_SPLIT
    tf = FREQ_TILE
    tap_block = lambda r: pl.BlockSpec((m, taps.shape[1]), lambda i: (r, 0), pipeline_mode=pl.Buffered(1))
    return pl.pallas_call(
        _kdft_kernel,
        out_shape=jax.ShapeDtypeStruct((m // tf, 8, tf, 2 * HY_W), F32),
        grid=(m // tf,),
        in_specs=[pl.BlockSpec((2 * tf, m), lambda i: (i, 0))] + [tap_block(r) for r in range(TIME_SPLIT)]
        + [_const_spec(mass.shape), pl.BlockSpec((tf, tw.shape[1]), lambda i: (i, 0))],
        out_specs=pl.BlockSpec((1, 8, tf, 2 * HY_W), lambda i: (i, 0, 0, 0)),
        compiler_params=_params(1),
        name="kdft",
    )(fwd, taps, taps, taps, taps, mass, tw)


HYENA_SUB = 128


def _short_conv_split(x_ref, cw, cb):
    x = [x_ref[0, r].astype(F32) for r in range(TIME_SPLIT)]
    m = x[0].shape[0]
    t = lax.broadcasted_iota(jnp.int32, (m, 1), 0)
    before = jnp.where(t == 0, 0.0, pltpu.roll(x[-1], 1, 0))
    after = jnp.where(t == m - 1, 0.0, pltpu.roll(x[0], m - 1, 0))
    prev = [before] + x[:-1]
    nxt = x[1:] + [after]
    return [cb + prev[r] * cw[0:1, :] + x[r] * cw[1:2, :] + nxt[r] * cw[2:3, :] for r in range(TIME_SPLIT)]


def _hyena_kernel(hv_ref, hx_ref, cw_ref, cb_ref, hb_ref, tw_ref, fwd_ref, inv_ref, kt_ref,
                  o_ref, u_ref, acc_ref, y_ref, nat_ref):
    order = pl.program_id(1)
    i = pl.program_id(2)
    tf = FREQ_TILE
    m, wide = u_ref.shape
    w = wide // TIME_SPLIT
    last = m // tf - 1
    lanes = [slice(r * w, (r + 1) * w) for r in range(TIME_SPLIT)]

    def restart(parts, bias):
        for r, part in enumerate(parts):
            u_ref[:, lanes[r]] = part.astype(BF16)
            acc_ref[:, lanes[r]] = bias * part

    @pl.when((order == 0) & (i == 0))
    def _():
        restart(_short_conv_split(hv_ref, cw_ref[0], cb_ref[0]), hb_ref[0:1, :])

    xs = [u_ref[:, ln] for ln in lanes]
    for sb in range(tf // HYENA_SUB):
        rc = slice(sb * HYENA_SUB, (sb + 1) * HYENA_SUB)
        rs = slice(tf + sb * HYENA_SUB, tf + (sb + 1) * HYENA_SUB)
        tw = tw_ref[rc, :]
        x = _split4_forward(fwd_ref[rc, :], fwd_ref[rs, :], xs, tw)
        y = []
        for k in range(4):
            kr, ki = kt_ref[0, 2 * k, rc, :], kt_ref[0, 2 * k + 1, rc, :]
            y += [kr * x[2 * k] + ki * x[2 * k + 1], kr * x[2 * k + 1] - ki * x[2 * k]]
        for r, (yc, ys) in enumerate(_split4_inverse(y, tw)):
            y_ref[rc, lanes[r]] = yc.astype(BF16)
            y_ref[rs, lanes[r]] = ys.astype(BF16)
    acc_ref[...] += _dot(inv_ref[0], y_ref[...])

    @pl.when((order == 0) & (i == last))
    def _():
        gates = _short_conv_split(hx_ref, cw_ref[1], cb_ref[1])
        restart([g * acc_ref[:, ln] for g, ln in zip(gates, lanes)], hb_ref[1:2, :])

    @pl.when((order == 1) & (i == last))
    def _():
        gates = _short_conv_split(hx_ref, cw_ref[2], cb_ref[2])
        z = [g * acc_ref[:, ln] for g, ln in zip(gates, lanes)]
        for k in range(w // LANES):
            cols = slice(k * LANES, (k + 1) * LANES)
            for r in range(TIME_SPLIT):
                nat_ref[k, pl.ds(r, m, stride=TIME_SPLIT), :] = z[r][:, cols]
            o_ref[0, :, cols] = nat_ref[k].astype(BF16)


def _hyena(hy, conv_w, conv_b, hy_bias, tw, fwd, inv, ktab):
    b, _, m, _ = hy.shape
    tf = FREQ_TILE
    nf = m // tf
    cw = conv_w.reshape(conv_w.shape[0], 3, HY_W).transpose(1, 0, 2)
    cb = conv_b.reshape(3, 1, HY_W)
    return pl.pallas_call(
        _hyena_kernel,
        out_shape=jax.ShapeDtypeStruct((b, TIME_SPLIT * m, HY_W), BF16),
        grid=(b, 2, nf),
        in_specs=[
            pl.BlockSpec((1, TIME_SPLIT, m, HY_W), lambda bi, o, i: (bi, 0, 0, 0)),
            pl.BlockSpec((1, TIME_SPLIT, m, HY_W), lambda bi, o, i: (bi, 0, 0, 1 + o)),
            _const_spec(cw.shape), _const_spec(cb.shape), _const_spec(hy_bias.shape),
            pl.BlockSpec((tf, tw.shape[1]), lambda bi, o, i: (i, 0)),
            pl.BlockSpec((2 * tf, m), lambda bi, o, i: (i, 0)),
            pl.BlockSpec((1, m, 2 * tf), lambda bi, o, i: (i, 0, 0)),
            pl.BlockSpec((1, 8, tf, HY_W), lambda bi, o, i: (i, 0, 0, o)),
        ],
        out_specs=pl.BlockSpec((1, TIME_SPLIT * m, HY_W), lambda bi, o, i: (bi, 0, 0)),
        scratch_shapes=[pltpu.VMEM((m, TIME_SPLIT * HY_W), BF16), pltpu.VMEM((m, TIME_SPLIT * HY_W), F32),
                        pltpu.VMEM((2 * tf, TIME_SPLIT * HY_W), BF16),
                        pltpu.VMEM((HY_W // LANES, TIME_SPLIT * m, LANES), F32)],
        compiler_params=_params(3),
        name="hyena",
    )(hy, hy, cw, cb, hy_bias, tw, fwd, inv, ktab)


def _grid_pos_embed(n_tokens):
    rows = n_tokens // GRID_W
    quarter = D_MODEL // 4
    omega = 1.0 / (10000.0 ** (jnp.arange(quarter, dtype=F32) / quarter))
    ar = jnp.arange(rows, dtype=F32)[:, None] * omega
    ac = jnp.arange(GRID_W, dtype=F32)[:, None] * omega
    er = jnp.concatenate([jnp.sin(ar), jnp.cos(ar)], axis=-1)
    ec = jnp.concatenate([jnp.sin(ac), jnp.cos(ac)], axis=-1)
    emb = jnp.concatenate([jnp.broadcast_to(er[:, None, :], (rows, GRID_W, D_MODEL // 2)),
                           jnp.broadcast_to(ec[None, :, :], (rows, GRID_W, D_MODEL // 2))], axis=-1)
    return emb.reshape(rows * GRID_W, D_MODEL)


def kernel(x, c, ctx, c_ctx, mod_w, mod_b, ffn_w_gate, ffn_w_up, ffn_w_down, w_in, hgrn_lb_logits,
           hgrn_norm_w, hyena_conv_w, hyena_conv_b, hyena_w1, hyena_b1, hyena_freq1, hyena_w2, hyena_b2,
           hyena_freq2, hyena_w3, hyena_bias, w_proj_a, w_proj_b, w_out, final_norm_w):
    assert mod_w.shape[0] == 1, "single-layer configuration"
    batch, n_lat, d = x.shape

    c_all = jnp.concatenate([c, c_ctx[None, :]], axis=0)
    c_all = jnp.pad(c_all, ((0, -c_all.shape[0] % 8), (0, 0)))
    m3 = _modulation(c_all, mod_w[0], mod_b[0][None, :]).reshape(c_all.shape[0], N_MOD, d)

    lb = jnp.cumsum(jax.nn.softmax(hgrn_lb_logits.astype(F32), axis=0), axis=0)[0]
    wg, wu, wd = _to_bf16(ffn_w_gate[0]), _to_bf16(ffn_w_up[0]), _to_bf16(ffn_w_down[0])
    w_in_b = _to_bf16(w_in[0])

    h1 = _half_ffn(x, m3, wg, wu, wd, 0, mod_base=0, pos=_grid_pos_embed(n_lat))
    n_ctx = ctx.shape[1]
    slab = math.gcd(batch * n_ctx, FFN_TILE)
    hc1 = _half_ffn(ctx.reshape(-1, slab, d), m3, wg, wu, wd, 0, mod_base=0, mod_row=batch)

    vc, lffc, lfbc = (a.reshape(batch, n_ctx, a.shape[-1])
                      for a in _input_proj(hc1, m3, w_in_b, lb, mod_row=batch))
    v, lff, lfb, q, g, hy, sg = _input_proj(h1, m3, w_in_b, lb)
    o_a = _hgrn2(q, v, lff, lfb, g, vc, lffc, lfbc, hgrn_norm_w[0][None, :])

    taps, mass = _hyena_filters(n_lat, hyena_w1[0], hyena_b1[0], hyena_freq1[0], hyena_w2[0], hyena_b2[0],
                                hyena_freq2[0], hyena_w3[0])
    tw = _twiddles(n_lat)
    fwd, inv = _odd_dft_matrices(n_lat // TIME_SPLIT)
    ktab = _filter_spectrum(fwd, taps, mass, tw)
    o_b = _hyena(hy, hyena_conv_w[0], hyena_conv_b[0], hyena_bias[0], tw, fwd, inv, ktab)

    mixers = (o_a, o_b, sg, _to_bf16(w_proj_a[0]), _to_bf16(w_proj_b[0]), _to_bf16(w_out[0]))
    return _half_ffn(h1, m3, wg, wu, wd, 1, mod_base=6, mixers=mixers, final_norm_w=final_norm_w[None, :])
```

```python
import functools
import math

import jax
import jax.numpy as jnp
from jax import lax
from jax.experimental import pallas as pl
from jax.experimental.pallas import tpu as pltpu

F32 = jnp.float32
BF16 = jnp.bfloat16
HIGHEST = lax.Precision.HIGHEST

D_MODEL = 1024
GRID_W = 64
HEADS = 4
HEAD_DIM = 128
KW = HEADS * HEAD_DIM
HY_W = 512
D_FF = 2816
N_MOD = 9
RMS_EPS = 1e-6
HYENA_EMB = 33
HYENA_BANDS = (HYENA_EMB - 1) // 2
HYENA_FFN = 64
HYENA_FAST_DECAY = 0.3
HYENA_SLOW_DECAY = 1.5
HYENA_TARGET = 1e-2
HYENA_SHIFT = 0.05
HYENA_L1_EPS = 1e-6

COL_V = 0
COL_FFW = COL_V + KW
COL_FBW = COL_FFW + KW
COL_Q = COL_FBW + KW
COL_G = COL_Q + KW
COL_HY = COL_G + KW
COL_MERGE = COL_HY + 3 * HY_W
IN_COLS = COL_MERGE + 2 * D_MODEL

V7X_VMEM_BYTES = 64 * 1024 * 1024
VMEM_LIMIT = V7X_VMEM_BYTES - 8 * 1024 * 1024

TOKEN_TILE = 512
PROJ_TILE = 1024
FFN_TILE = 1024
FF_CHUNK = 256
SCAN_CHUNK = 128
SCAN_UNROLL = 2
FREQ_TILE = 256
TIME_SPLIT = 4
FILT_TILE = 256
FEAT_PAD = 128
LANES = 128


def _const_spec(shape):
    nd = len(shape)
    return pl.BlockSpec(shape, lambda *_: (0,) * nd, pipeline_mode=pl.Buffered(1))


def _params(n_grid):
    return pltpu.CompilerParams(dimension_semantics=("arbitrary",) * n_grid, vmem_limit_bytes=VMEM_LIMIT)


def _rms(x):
    return x * lax.rsqrt(jnp.mean(x * x, axis=-1, keepdims=True) + RMS_EPS)


def _norm_mod(h, shift, scale):
    return _rms(h) * (1.0 + scale) + shift


def _dot(a, b):
    return jnp.dot(a, b, preferred_element_type=F32)


CAST_BLOCK_BYTES = 4 * 1024 * 1024


def _cast_kernel(x_ref, o_ref):
    o_ref[...] = x_ref[...].astype(BF16)


def _to_bf16(w):
    w2 = w.reshape(-1, w.shape[-1])
    r, c = w2.shape
    tr = next(r // k for k in range(1, r + 1)
              if r % k == 0 and (r // k) % 8 == 0 and (r // k) * c * 4 <= CAST_BLOCK_BYTES)
    out = pl.pallas_call(
        _cast_kernel,
        out_shape=jax.ShapeDtypeStruct((r, c), BF16),
        grid=(r // tr,),
        in_specs=[pl.BlockSpec((tr, c), lambda i: (i, 0))],
        out_specs=pl.BlockSpec((tr, c), lambda i: (i, 0)),
        compiler_params=_params(1),
        name="cast",
    )(w2)
    return out.reshape(w.shape)


def _mod_kernel(c_ref, w_ref, b_ref, o_ref):
    c = c_ref[...]
    a = c * jax.nn.sigmoid(c)
    o_ref[...] = _dot(a.astype(BF16), w_ref[...].astype(BF16)) + b_ref[...]


def _modulation(c_all, mod_w, mod_b):
    rows = c_all.shape[0]
    tn = D_MODEL
    return pl.pallas_call(
        _mod_kernel,
        out_shape=jax.ShapeDtypeStruct((rows, N_MOD * D_MODEL), F32),
        grid=(N_MOD,),
        in_specs=[
            pl.BlockSpec((rows, D_MODEL), lambda j: (0, 0)),
            pl.BlockSpec((D_MODEL, tn), lambda j: (0, j)),
            pl.BlockSpec((1, tn), lambda j: (0, j)),
        ],
        out_specs=pl.BlockSpec((rows, tn), lambda j: (0, j)),
        compiler_params=_params(1),
        name="mod",
    )(c_all, mod_w, mod_b)


def _ffn_kernel(*refs, mod_base, add_pos, mixers, final_norm):
    refs = list(refs)
    h_ref = refs.pop(0)
    pos_ref = refs.pop(0) if add_pos else None
    m_ref = refs.pop(0)
    if mixers:
        oa_ref, ob_ref, sg_ref, wpa_ref, wpb_ref, wo_ref = refs[:6]
        refs = refs[6:]
    wg_ref, wu_ref, wd_ref = refs[:3]
    refs = refs[3:]
    fnw_ref = refs.pop(0) if final_norm else None
    o_ref = refs.pop(0)

    h = h_ref[0]
    if add_pos:
        h = h + pos_ref[...]
    if mixers:
        d = h.shape[1]
        ya = _dot(oa_ref[0], wpa_ref[...])
        yb = _dot(ob_ref[0], wpb_ref[...])
        y = sg_ref[0, :, :d].astype(F32) * ya + sg_ref[0, :, d:].astype(F32) * yb
        h = h + m_ref[0, 5:6, :] * _dot(y.astype(BF16), wo_ref[...])
    shift = m_ref[0, mod_base:mod_base + 1, :]
    scale = m_ref[0, mod_base + 1:mod_base + 2, :]
    gate = m_ref[0, mod_base + 2:mod_base + 3, :]
    nb = _norm_mod(h, shift, scale).astype(BF16)
    acc = jnp.zeros(h.shape, F32)
    for c in range(D_FF // FF_CHUNK):
        sl = slice(c * FF_CHUNK, (c + 1) * FF_CHUNK)
        g = _dot(nb, wg_ref[0, :, sl])
        u = _dot(nb, wu_ref[0, :, sl])
        a = (g * jax.nn.sigmoid(g) * u).astype(BF16)
        acc = acc + _dot(a, wd_ref[0, sl, :])
    out = h + 0.5 * gate * acc
    if final_norm:
        out = _rms(out) * fnw_ref[...]
    o_ref[0] = out


def _half_ffn(h, m3, wg, wu, wd, half, *, mod_base, mod_row=None, pos=None, mixers=None, final_norm_w=None):
    b, l, d = h.shape
    tm = min(TOKEN_TILE if mixers is not None else FFN_TILE, l)
    nt = l // tm
    row_map = (lambda j, i: (i, 0, 0)) if mod_row is None else (lambda j, i: (mod_row, 0, 0))
    tok = lambda w: pl.BlockSpec((1, tm, w), lambda j, i: (i, j, 0))
    in_specs = [tok(d)]
    args = [h]
    if pos is not None:
        in_specs.append(pl.BlockSpec((tm, d), lambda j, i: (j, 0)))
        args.append(pos)
    in_specs.append(pl.BlockSpec((1, N_MOD, d), row_map))
    args.append(m3)
    if mixers is not None:
        in_specs += [tok(a.shape[2]) for a in mixers[:3]] + [_const_spec(w.shape) for w in mixers[3:]]
        args += list(mixers)
    in_specs += [pl.BlockSpec((1,) + w.shape[1:], lambda j, i: (half, 0, 0), pipeline_mode=pl.Buffered(1))
                 for w in (wg, wu, wd)]
    args += [wg, wu, wd]
    if final_norm_w is not None:
        in_specs.append(_const_spec(final_norm_w.shape))
        args.append(final_norm_w)
    kern = functools.partial(_ffn_kernel, mod_base=mod_base, add_pos=pos is not None,
                             mixers=mixers is not None, final_norm=final_norm_w is not None)
    return pl.pallas_call(
        kern,
        out_shape=jax.ShapeDtypeStruct((b, l, d), F32),
        grid=(nt, b),
        in_specs=in_specs,
        out_specs=tok(d),
        compiler_params=_params(2),
        name="ffn_mix" if mixers is not None else "ffn",
    )(*args)


def _log2_forget(z, lb):
    return jnp.log2(lb + (1.0 - lb) * jax.nn.sigmoid(z))


def _proj_kernel(h_ref, m_ref, w_ref, lb_ref, *out_refs, full):
    h = h_ref[0]
    nb = _norm_mod(h, m_ref[0, 3:4, :], m_ref[0, 4:5, :]).astype(BF16)

    def proj(c0):
        return _dot(nb, w_ref[:, c0:c0 + KW])

    v_ref, lff_ref, lfb_ref = out_refs[:3]
    v_ref[0] = proj(COL_V).astype(BF16)
    lff_ref[0] = _log2_forget(proj(COL_FFW), lb_ref[0:1, :])
    lfb_ref[0] = _log2_forget(proj(COL_FBW), lb_ref[1:2, :])
    if full:
        q_ref, g_ref, hy_ref, mg_ref, par_ref = out_refs[3:]
        zq = proj(COL_Q)
        q_ref[0] = (zq * jax.nn.sigmoid(zq)).astype(BF16)
        zg = proj(COL_G)
        g_ref[0] = (zg * jax.nn.sigmoid(zg)).astype(BF16)
        part = par_ref.shape[1] // TIME_SPLIT
        for k in range(3 * HY_W // KW):
            z = proj(COL_HY + k * KW)
            for c in range(KW // LANES):
                par_ref[c] = z[:, c * LANES:(c + 1) * LANES]
                cols = slice(k * KW + c * LANES, k * KW + (c + 1) * LANES)
                for r in range(TIME_SPLIT):
                    hy_ref[0, r, :, cols] = par_ref[c, pl.ds(r, part, stride=TIME_SPLIT), :].astype(BF16)
        for k in range(2 * D_MODEL // KW):
            mg_ref[0, :, k * KW:(k + 1) * KW] = jax.nn.sigmoid(proj(COL_MERGE + k * KW)).astype(BF16)


def _input_proj(h, m3, w_in, lb, *, mod_row=None):
    b, l, d = h.shape
    full = mod_row is None
    tm = min(PROJ_TILE, l)
    nt = l // tm
    row_map = (lambda j, i: (i, 0, 0)) if full else (lambda j, i: (mod_row, 0, 0))
    ncols = IN_COLS if full else COL_Q
    tok = lambda w: pl.BlockSpec((1, tm, w), lambda j, i: (i, j, 0))
    shapes = [(KW, BF16), (KW, F32), (KW, F32)]
    if full:
        shapes += [(KW, BF16), (KW, BF16), None, (2 * D_MODEL, BF16)]
    out_shape = [jax.ShapeDtypeStruct((b, l, s[0]), s[1]) if s else
                 jax.ShapeDtypeStruct((b, TIME_SPLIT, l // TIME_SPLIT, 3 * HY_W), BF16) for s in shapes]
    out_specs = [tok(s[0]) if s else
                 pl.BlockSpec((1, TIME_SPLIT, tm // TIME_SPLIT, 3 * HY_W), lambda j, i: (i, 0, j, 0))
                 for s in shapes]
    return pl.pallas_call(
        functools.partial(_proj_kernel, full=full),
        out_shape=out_shape,
        grid=(nt, b),
        in_specs=[
            tok(d),
            pl.BlockSpec((1, N_MOD, d), row_map),
            pl.BlockSpec((d, ncols), lambda j, i: (0, 0), pipeline_mode=pl.Buffered(1)),
            _const_spec(lb.shape),
        ],
        out_specs=out_specs,
        scratch_shapes=[pltpu.VMEM((KW // LANES, tm, LANES), F32)] if full else [],
        compiler_params=_params(2),
        name="proj" if full else "proj_ctx",
    )(h, m3, w_in, lb)


def _split2(x):
    hi = x.astype(BF16)
    lo = (x - hi.astype(F32)).astype(BF16)
    return hi, lo


def _dot_nt(a, b):
    return lax.dot_general(a, b, (((1,), (1,)), ((), ())), preferred_element_type=F32)


def _scan_kernel(q_ref, v_ref, lff_ref, lfb_ref, g_ref, vc_ref, lffc_ref, lfbc_ref, nw_ref,
                 o_ref, oacc_ref, vt_ref, vtc_ref, st_ref):
    c = SCAN_CHUNK
    n_lat = q_ref.shape[1] // c
    n_ctx = vc_ref.shape[1] // c
    row = lax.broadcasted_iota(jnp.int32, (c, c), 0)
    col = lax.broadcasted_iota(jnp.int32, (c, c), 1)
    lower = row >= col
    upper = row <= col
    tri_l = jnp.where(lower, 1.0, 0.0).astype(BF16)
    tri_u = jnp.where(upper, 1.0, 0.0).astype(BF16)

    for j in range(n_lat):
        vt_ref[j] = v_ref[0, j * c:(j + 1) * c, :].astype(F32).T.astype(BF16)
    for j in range(n_ctx):
        vtc_ref[j] = vc_ref[0, j * c:(j + 1) * c, :].astype(F32).T.astype(BF16)

    heads = [slice(hd * HEAD_DIM, (hd + 1) * HEAD_DIM) for hd in range(HEADS)]
    half = c // 2

    def cumulate(direction, lf):
        hi, lo = _split2(lf)
        tri = tri_l if direction == 0 else tri_u
        return _dot(tri, hi) + _dot(tri, lo)

    def prepare(direction, lf, q, b=None):
        i1, i2, tot_i = (half // 2 - 1, half + half // 2 - 1, c - 1) if direction == 0 else \
            (half // 2, half + half // 2, 0)
        b = cumulate(direction, lf) if b is None else b
        k = 1.0 - jnp.exp2(lf)
        tot = b[tot_i:tot_i + 1, :]
        kd = (k * jnp.exp2(tot - b)).astype(BF16)
        dec = jnp.exp2(tot)
        if q is None:
            return kd, dec, None, None, None, None
        r1, r2 = b[i1:i1 + 1, :], b[i2:i2 + 1, :]
        ref = jnp.concatenate([jnp.broadcast_to(r1, (half, KW)), jnp.broadcast_to(r2, (half, KW))], axis=0)
        qf = q.astype(F32)
        qm = qf * jnp.exp2(b - ref)
        qd = (qf * jnp.exp2(b)).astype(BF16)
        km = (k * jnp.exp2(ref - b)).astype(BF16)
        if direction == 0:
            qx = qm[half:] * jnp.exp2(r2 - r1)
        else:
            qx = qm[:half] * jnp.exp2(r1 - r2)
        return kd, dec, qm.astype(BF16), qd, km, qx.astype(BF16)

    def advance(items):
        with_q = [p[2] is not None for _, p, _, _ in items]
        zero = jnp.zeros((half, HEAD_DIM), BF16)

        def pair_scores(d, qm, km, qx):
            q1, q2, k1, k2 = qm[:half], qm[half:], km[:half], km[half:]
            cat = lambda rows: jnp.concatenate([jnp.concatenate(r, axis=1) for r in rows], axis=0)
            if d == 0:
                return _dot_nt(cat([[q1, zero, zero], [zero, q2, qx]]), cat([[k1, zero, k1], [zero, k2, zero]]))
            return _dot_nt(cat([[q1, zero, qx], [zero, q2, zero]]), cat([[k1, zero, zero], [zero, k2, k2]]))

        scores = [[pair_scores(d, p[2][:, hs], p[4][:, hs], p[5][:, hs]) for hs in heads] if wq else None
                  for (d, p, _, _), wq in zip(items, with_q)]
        grow = [[_dot(vt[hs, :], p[0][:, hs]) for hs in heads] for _, p, vt, _ in items]
        state = {d: [st_ref[d, hd] for hd in range(HEADS)] for d in {d for d, *_ in items}}
        carry = []
        for n, (d, p, _, _) in enumerate(items):
            carry.append([_dot_nt(p[3][:, hs], state[d][hd].astype(BF16)) for hd, hs in enumerate(heads)]
                         if with_q[n] else None)
            state[d] = [state[d][hd] * p[1][:, hs] + grow[n][hd] for hd, hs in enumerate(heads)]
        for d, sts in state.items():
            for hd in range(HEADS):
                st_ref[d, hd] = sts[hd]
        outs = []
        for n, (d, _, _, v) in enumerate(items):
            if not with_q[n]:
                outs.append(None)
                continue
            mask = lower if d == 0 else upper
            o = [_dot(jnp.where(mask, scores[n][hd], 0.0).astype(BF16), v[:, hs]) + carry[n][hd]
                 for hd, hs in enumerate(heads)]
            outs.append(jnp.concatenate(o, axis=-1))
        return outs

    st_ref[...] = jnp.zeros(st_ref.shape, F32)
    advance([(0, prepare(0, lffc_ref[0, j * c:(j + 1) * c, :], None), vtc_ref[j], None) for j in range(n_ctx)]
            + [(1, prepare(1, lfbc_ref[0, j * c:(j + 1) * c, :], None), vtc_ref[j], None)
               for j in reversed(range(n_ctx))])

    nw = nw_ref[...]

    def finish(rows, o):
        normed = [_rms(o[:, hs]) * nw for hs in heads]
        o_ref[0, rows, :] = (jnp.concatenate(normed, axis=-1) * g_ref[0, rows, :].astype(F32)).astype(BF16)

    unroll = SCAN_UNROLL
    n_steps = n_lat // unroll

    def step_chunks(j):
        idx = [unroll * j + u for u in range(unroll)] + [n_lat - 1 - (unroll * j + u) for u in range(unroll)]
        start = (lambda i: i * c) if isinstance(j, int) else (lambda i: pl.multiple_of(i * c, c))
        return [(0 if n < unroll else 1, i, pl.ds(start(i), c)) for n, i in enumerate(idx)]

    def log_forget(d, r):
        return (lff_ref if d == 0 else lfb_ref)[0, r, :]

    def step(j, final):
        todo = step_chunks(j)
        cums = [cumulate(d, log_forget(d, r)) for d, _, r in todo]
        ops = [prepare(d, log_forget(d, r), q_ref[0, r, :], b) for (d, _, r), b in zip(todo, cums)]
        outs = advance([(d, p, vt_ref[i], v_ref[0, r, :]) for (d, i, r), p in zip(todo, ops)])
        for (_, _, r), o in zip(todo, outs):
            if final:
                finish(r, oacc_ref[r, :] + o)
            else:
                oacc_ref[r, :] = o

    def first_half(j, carry):
        step(j, False)
        return carry

    def second_half(j, carry):
        step(j, True)
        return carry

    lax.fori_loop(0, n_steps // 2, first_half, 0)
    lax.fori_loop(n_steps // 2, n_steps, second_half, 0)


def _hgrn2(q, v, lff, lfb, g, vc, lffc, lfbc, norm_w):
    b, l, _ = q.shape
    lc = vc.shape[1]
    c = SCAN_CHUNK
    seq = lambda n: pl.BlockSpec((1, n, KW), lambda i: (i, 0, 0))
    return pl.pallas_call(
        _scan_kernel,
        out_shape=jax.ShapeDtypeStruct((b, l, KW), BF16),
        grid=(b,),
        in_specs=[seq(l), seq(l), seq(l), seq(l), seq(l), seq(lc), seq(lc), seq(lc),
                  _const_spec(norm_w.shape)],
        out_specs=seq(l),
        scratch_shapes=[pltpu.VMEM((l, KW), F32), pltpu.VMEM((l // c, KW, c), BF16),
                        pltpu.VMEM((lc // c, KW, c), BF16), pltpu.VMEM((2, HEADS, HEAD_DIM, HEAD_DIM), F32)],
        compiler_params=_params(1),
        name="scan",
    )(q, v, lff, lfb, g, vc, lffc, lfbc, norm_w)


def _filter_kernel(z_ref, w1_ref, b1_ref, f1_ref, w2_ref, b2_ref, f2_ref, w3_ref, win_ref, h_ref, mass_ref):
    hp = dict(precision=HIGHEST, preferred_element_type=F32)
    h = jnp.sin(f1_ref[...] * (jnp.dot(z_ref[...], w1_ref[...], **hp) + b1_ref[...]))
    h = jnp.sin(f2_ref[...] * (jnp.dot(h, w2_ref[...], **hp) + b2_ref[...]))
    (h_hi, h_lo), (w_hi, w_lo) = _split2(h), _split2(w3_ref[...])
    h = _dot(h_hi, w_hi) + _dot(h_hi, w_lo) + _dot(h_lo, w_hi)
    win = win_ref[...]
    n_groups = h.shape[1] // HY_W
    h = jnp.concatenate([h[:, k * HY_W:(k + 1) * HY_W] * win for k in range(n_groups)], axis=-1)
    h_ref[...] = h.astype(BF16)

    @pl.when(pl.program_id(0) == 0)
    def _():
        mass_ref[...] = jnp.zeros(mass_ref.shape, F32)

    mass_ref[...] += jnp.sum(jnp.abs(h), axis=0, keepdims=True)


def _hyena_filters(l, w1, b1, fr1, w2, b2, fr2, w3):
    p = jnp.concatenate([jnp.arange(r, l, TIME_SPLIT) for r in range(TIME_SPLIT)]).astype(F32)
    t = p / (l - 1)
    w = 2.0 * math.pi * p / l
    f = jnp.linspace(1e-4, HYENA_BANDS - 1, HYENA_BANDS, dtype=F32)
    ang = w[:, None] * f[None, :]
    z = jnp.concatenate([t[:, None], jnp.cos(ang), -jnp.sin(ang)], axis=-1)
    max_decay = math.log(HYENA_TARGET) / HYENA_FAST_DECAY
    min_decay = math.log(HYENA_TARGET) / HYENA_SLOW_DECAY
    deltas = jnp.abs(jnp.linspace(min_decay, max_decay, HY_W, dtype=F32))
    window = jnp.exp(-t[:, None] * deltas[None, :]) + HYENA_SHIFT

    pad_c = lambda a, n: jnp.pad(a, ((0, 0), (0, n - a.shape[1])))
    pad_r = lambda a, n: jnp.pad(a, ((0, n - a.shape[0]), (0, 0)))
    z = pad_c(z, FEAT_PAD)
    w1p = pad_c(pad_r(w1, FEAT_PAD), FEAT_PAD)
    w2p = pad_c(pad_r(w2, FEAT_PAD), FEAT_PAD)
    w3p = pad_r(w3, FEAT_PAD)
    vec = lambda a: pad_c(a[None, :], FEAT_PAD)
    n_out = w3.shape[1]
    tl = FILT_TILE
    return pl.pallas_call(
        _filter_kernel,
        out_shape=[jax.ShapeDtypeStruct((l, n_out), BF16), jax.ShapeDtypeStruct((1, n_out), F32)],
        grid=(l // tl,),
        in_specs=[
            pl.BlockSpec((tl, FEAT_PAD), lambda i: (i, 0)),
            _const_spec(w1p.shape), _const_spec((1, FEAT_PAD)), _const_spec((1, FEAT_PAD)),
            _const_spec(w2p.shape), _const_spec((1, FEAT_PAD)), _const_spec((1, FEAT_PAD)),
            _const_spec(w3p.shape),
            pl.BlockSpec((tl, HY_W), lambda i: (i, 0)),
        ],
        out_specs=[pl.BlockSpec((tl, n_out), lambda i: (i, 0)), pl.BlockSpec((1, n_out), lambda i: (0, 0))],
        compiler_params=_params(1),
        name="filt",
    )(z, w1p, vec(b1), vec(fr1), w2p, vec(b2), vec(fr2), w3p, window)


def _dft_kernel(ar_ref, ai_ref, br_ref, bi_ref, cr_ref, ci_ref, dr_ref, di_ref, fwd_ref, inv_ref):
    tf = FREQ_TILE
    br, bi = br_ref[...], bi_ref[...]
    for t1 in range(fwd_ref.shape[1] // LANES):
        ar, ai = ar_ref[:, t1:t1 + 1], ai_ref[:, t1:t1 + 1]
        cols = slice(t1 * LANES, (t1 + 1) * LANES)
        fwd_ref[0:tf, cols] = (ar * br - ai * bi).astype(BF16)
        fwd_ref[tf:2 * tf, cols] = (ar * bi + ai * br).astype(BF16)
    dr, di = dr_ref[...], di_ref[...]
    for f1 in range(tf // LANES):
        cr, ci = cr_ref[0, :, f1:f1 + 1], ci_ref[0, :, f1:f1 + 1]
        inv_ref[0, :, f1 * LANES:(f1 + 1) * LANES] = (cr * dr - ci * di).astype(BF16)
        inv_ref[0, :, tf + f1 * LANES:tf + (f1 + 1) * LANES] = (cr * di + ci * dr).astype(BF16)


def _odd_dft_matrices(l):
    n_ang = 4 * l
    theta = 2.0 * math.pi / n_ang
    tf = FREQ_TILE
    nf = l // tf
    n_hi = l // LANES

    def cis(idx):
        a = (idx % n_ang).astype(F32) * theta
        return jnp.cos(a), jnp.sin(a)

    idx = jnp.arange(l, dtype=jnp.int32)
    odd = 2 * idx + 1
    lane = jnp.arange(LANES, dtype=jnp.int32)
    ar, ai = cis(odd[:, None] * (LANES * jnp.arange(n_hi, dtype=jnp.int32))[None, :])
    br, bi = cis(odd[:, None] * lane[None, :])
    g = jnp.arange(n_hi, dtype=jnp.int32).reshape(nf, 1, tf // LANES)
    cr, ci = cis(idx[None, :, None] * (2 * LANES * g))
    dr, di = cis(idx[:, None] * (2 * lane + 1)[None, :])
    row_tab = lambda w: pl.BlockSpec((tf, w), lambda i: (i, 0))
    return pl.pallas_call(
        _dft_kernel,
        out_shape=[jax.ShapeDtypeStruct((2 * l, l), BF16), jax.ShapeDtypeStruct((nf, l, 2 * tf), BF16)],
        grid=(nf,),
        in_specs=[row_tab(n_hi), row_tab(n_hi), row_tab(LANES), row_tab(LANES),
                  pl.BlockSpec((1, l, tf // LANES), lambda i: (i, 0, 0)),
                  pl.BlockSpec((1, l, tf // LANES), lambda i: (i, 0, 0)),
                  _const_spec((l, LANES)), _const_spec((l, LANES))],
        out_specs=[pl.BlockSpec((2 * tf, l), lambda i: (i, 0)), pl.BlockSpec((1, l, 2 * tf), lambda i: (i, 0, 0))],
        compiler_params=_params(1),
        name="dft",
    )(ar, ai, br, bi, cr, ci, dr, di)


def _butterfly(ac, as_, bc, bs, cph, sph):
    tc = cph * bc - sph * bs
    ts = cph * bs + sph * bc
    return ac + tc, as_ + ts, ac - tc, ts - as_


def _inv_butterfly(yc, ys, yhc, yhs, cph, sph):
    dc, ds = yc - yhc, ys + yhs
    return yc + yhc, ys - yhs, dc * cph + ds * sph, ds * cph - dc * sph


def _split4_forward(fc, fs, xs, tw):
    c1, s1, c1g, s1g, c2, s2 = (tw[:, k:k + 1] for k in range(6))
    p = [(_dot(fc, x), _dot(fs, x)) for x in xs]
    ev = _butterfly(*p[0], *p[2], c2, s2)
    od = _butterfly(*p[1], *p[3], c2, s2)
    return _butterfly(*ev[:2], *od[:2], c1, s1) + _butterfly(*ev[2:], *od[2:], c1g, s1g)


def _split4_inverse(y, tw):
    c1, s1, c1g, s1g, c2, s2 = (tw[:, k:k + 1] for k in range(6))
    at_f = _inv_butterfly(*y[0:4], c1, s1)
    at_g = _inv_butterfly(*y[4:8], c1g, s1g)
    r0c, r0s, r2c, r2s = _inv_butterfly(*at_f[:2], *at_g[:2], c2, s2)
    r1c, r1s, r3c, r3s = _inv_butterfly(*at_f[2:], *at_g[2:], c2, s2)
    return [(r0c, r0s), (r1c, r1s), (r2c, r2s), (r3c, r3s)]


def _kdft_kernel(fwd_ref, t0_ref, t1_ref, t2_ref, t3_ref, mass_ref, tw_ref, o_ref):
    tf = FREQ_TILE
    taps = (t0_ref, t1_ref, t2_ref, t3_ref)
    l = TIME_SPLIT * t0_ref.shape[0]
    fc, fs = fwd_ref[0:tf, :], fwd_ref[tf:2 * tf, :]
    tw = tw_ref[...]
    f = pl.program_id(0) * tf + lax.broadcasted_iota(jnp.int32, (tf, 1), 0)
    sgn_f = jnp.where(f % 2 == 0, 1.0, -1.0)
    signs = (sgn_f, -sgn_f, -sgn_f, sgn_f)
    for order in range(2):
        fw = slice((2 * order) * HY_W, (2 * order + 1) * HY_W)
        bw = slice((2 * order + 1) * HY_W, (2 * order + 2) * HY_W)
        a = _split4_forward(fc, fs, [t[:, fw] for t in taps], tw)
        h = _split4_forward(fc, fs, [t[:, bw] for t in taps], tw)
        scale = (1.0 / l) / (mass_ref[:, fw] + mass_ref[:, bw] + HYENA_L1_EPS)
        cols = slice(order * HY_W, (order + 1) * HY_W)
        for k in range(4):
            ac, as_, hc, hs = a[2 * k], a[2 * k + 1], h[2 * k], h[2 * k + 1]
            ec, es = tw[:, 6 + 2 * k:7 + 2 * k], tw[:, 7 + 2 * k:8 + 2 * k]
            o_ref[0, 2 * k, :, cols] = (ac + signs[k] * (es * hc - ec * hs)) * scale
            o_ref[0, 2 * k + 1, :, cols] = (-as_ + signs[k] * (ec * hc + es * hs)) * scale


def _twiddles(l):
    n_ang = 4 * l
    theta = 2.0 * math.pi / n_ang
    f = jnp.arange(l // 4, dtype=jnp.int32)
    g = l // 2 - 1 - f
    odd = lambda x: 2 * x + 1
    angles = [odd(f), odd(g), 2 * odd(f)] + [odd(x) * (l - 1) for x in (f, l - 1 - f, g, l - 1 - g)]
    cols = []
    for a in angles:
        r = (a % n_ang).astype(F32) * theta
        cols += [jnp.cos(r), jnp.sin(r)]
    cols += [jnp.zeros_like(cols[0])] * 2
    return jnp.stack(cols, axis=-1)


def _filter_spectrum(fwd, taps, mass, tw):
    l = taps.shape[0]
    m = l // TIME_SPLIT
    tf = FREQ_TILE
    tap_block = lambda r: pl.BlockSpec((m, taps.shape[1]), lambda i: (r, 0), pipeline_mode=pl.Buffered(1))
    return pl.pallas_call(
        _kdft_kernel,
        out_shape=jax.ShapeDtypeStruct((m // tf, 8, tf, 2 * HY_W), F32),
        grid=(m // tf,),
        in_specs=[pl.BlockSpec((2 * tf, m), lambda i: (i, 0))] + [tap_block(r) for r in range(TIME_SPLIT)]
        + [_const_spec(mass.shape), pl.BlockSpec((tf, tw.shape[1]), lambda i: (i, 0))],
        out_specs=pl.BlockSpec((1, 8, tf, 2 * HY_W), lambda i: (i, 0, 0, 0)),
        compiler_params=_params(1),
        name="kdft",
    )(fwd, taps, taps, taps, taps, mass, tw)


HYENA_SUB = 128


def _short_conv_split(x_ref, cw, cb):
    x = [x_ref[0, r].astype(F32) for r in range(TIME_SPLIT)]
    m = x[0].shape[0]
    t = lax.broadcasted_iota(jnp.int32, (m, 1), 0)
    before = jnp.where(t == 0, 0.0, pltpu.roll(x[-1], 1, 0))
    after = jnp.where(t == m - 1, 0.0, pltpu.roll(x[0], m - 1, 0))
    prev = [before] + x[:-1]
    nxt = x[1:] + [after]
    return [cb + prev[r] * cw[0:1, :] + x[r] * cw[1:2, :] + nxt[r] * cw[2:3, :] for r in range(TIME_SPLIT)]


def _hyena_kernel(hv_ref, hx_ref, cw_ref, cb_ref, hb_ref, tw_ref, fwd_ref, inv_ref, kt_ref,
                  o_ref, u_ref, acc_ref, y_ref, nat_ref):
    order = pl.program_id(1)
    i = pl.program_id(2)
    tf = FREQ_TILE
    m, wide = u_ref.shape
    w = wide // TIME_SPLIT
    last = m // tf - 1
    lanes = [slice(r * w, (r + 1) * w) for r in range(TIME_SPLIT)]

    def restart(parts, bias):
        for r, part in enumerate(parts):
            u_ref[:, lanes[r]] = part.astype(BF16)
            acc_ref[:, lanes[r]] = bias * part

    @pl.when((order == 0) & (i == 0))
    def _():
        restart(_short_conv_split(hv_ref, cw_ref[0], cb_ref[0]), hb_ref[0:1, :])

    xs = [u_ref[:, ln] for ln in lanes]
    for sb in range(tf // HYENA_SUB):
        rc = slice(sb * HYENA_SUB, (sb + 1) * HYENA_SUB)
        rs = slice(tf + sb * HYENA_SUB, tf + (sb + 1) * HYENA_SUB)
        tw = tw_ref[rc, :]
        x = _split4_forward(fwd_ref[rc, :], fwd_ref[rs, :], xs, tw)
        y = []
        for k in range(4):
            kr, ki = kt_ref[0, 2 * k, rc, :], kt_ref[0, 2 * k + 1, rc, :]
            y += [kr * x[2 * k] + ki * x[2 * k + 1], kr * x[2 * k + 1] - ki * x[2 * k]]
        for r, (yc, ys) in enumerate(_split4_inverse(y, tw)):
            y_ref[rc, lanes[r]] = yc.astype(BF16)
            y_ref[rs, lanes[r]] = ys.astype(BF16)
    acc_ref[...] += _dot(inv_ref[0], y_ref[...])

    @pl.when((order == 0) & (i == last))
    def _():
        gates = _short_conv_split(hx_ref, cw_ref[1], cb_ref[1])
        restart([g * acc_ref[:, ln] for g, ln in zip(gates, lanes)], hb_ref[1:2, :])

    @pl.when((order == 1) & (i == last))
    def _():
        gates = _short_conv_split(hx_ref, cw_ref[2], cb_ref[2])
        z = [g * acc_ref[:, ln] for g, ln in zip(gates, lanes)]
        for k in range(w // LANES):
            cols = slice(k * LANES, (k + 1) * LANES)
            for r in range(TIME_SPLIT):
                nat_ref[k, pl.ds(r, m, stride=TIME_SPLIT), :] = z[r][:, cols]
            o_ref[0, :, cols] = nat_ref[k].astype(BF16)


def _hyena(hy, conv_w, conv_b, hy_bias, tw, fwd, inv, ktab):
    b, _, m, _ = hy.shape
    tf = FREQ_TILE
    nf = m // tf
    cw = conv_w.reshape(conv_w.shape[0], 3, HY_W).transpose(1, 0, 2)
    cb = conv_b.reshape(3, 1, HY_W)
    return pl.pallas_call(
        _hyena_kernel,
        out_shape=jax.ShapeDtypeStruct((b, TIME_SPLIT * m, HY_W), BF16),
        grid=(b, 2, nf),
        in_specs=[
            pl.BlockSpec((1, TIME_SPLIT, m, HY_W), lambda bi, o, i: (bi, 0, 0, 0)),
            pl.BlockSpec((1, TIME_SPLIT, m, HY_W), lambda bi, o, i: (bi, 0, 0, 1 + o)),
            _const_spec(cw.shape), _const_spec(cb.shape), _const_spec(hy_bias.shape),
            pl.BlockSpec((tf, tw.shape[1]), lambda bi, o, i: (i, 0)),
            pl.BlockSpec((2 * tf, m), lambda bi, o, i: (i, 0)),
            pl.BlockSpec((1, m, 2 * tf), lambda bi, o, i: (i, 0, 0)),
            pl.BlockSpec((1, 8, tf, HY_W), lambda bi, o, i: (i, 0, 0, o)),
        ],
        out_specs=pl.BlockSpec((1, TIME_SPLIT * m, HY_W), lambda bi, o, i: (bi, 0, 0)),
        scratch_shapes=[pltpu.VMEM((m, TIME_SPLIT * HY_W), BF16), pltpu.VMEM((m, TIME_SPLIT * HY_W), F32),
                        pltpu.VMEM((2 * tf, TIME_SPLIT * HY_W), BF16),
                        pltpu.VMEM((HY_W // LANES, TIME_SPLIT * m, LANES), F32)],
        compiler_params=_params(3),
        name="hyena",
    )(hy, hy, cw, cb, hy_bias, tw, fwd, inv, ktab)


def _grid_pos_embed(n_tokens):
    rows = n_tokens // GRID_W
    quarter = D_MODEL // 4
    omega = 1.0 / (10000.0 ** (jnp.arange(quarter, dtype=F32) / quarter))
    ar = jnp.arange(rows, dtype=F32)[:, None] * omega
    ac = jnp.arange(GRID_W, dtype=F32)[:, None] * omega
    er = jnp.concatenate([jnp.sin(ar), jnp.cos(ar)], axis=-1)
    ec = jnp.concatenate([jnp.sin(ac), jnp.cos(ac)], axis=-1)
    emb = jnp.concatenate([jnp.broadcast_to(er[:, None, :], (rows, GRID_W, D_MODEL // 2)),
                           jnp.broadcast_to(ec[None, :, :], (rows, GRID_W, D_MODEL // 2))], axis=-1)
    return emb.reshape(rows * GRID_W, D_MODEL)


def kernel(x, c, ctx, c_ctx, mod_w, mod_b, ffn_w_gate, ffn_w_up, ffn_w_down, w_in, hgrn_lb_logits,
           hgrn_norm_w, hyena_conv_w, hyena_conv_b, hyena_w1, hyena_b1, hyena_freq1, hyena_w2, hyena_b2,
           hyena_freq2, hyena_w3, hyena_bias, w_proj_a, w_proj_b, w_out, final_norm_w):
    assert mod_w.shape[0] == 1, "single-layer configuration"
    batch, n_lat, d = x.shape

    c_all = jnp.concatenate([c, c_ctx[None, :]], axis=0)
    c_all = jnp.pad(c_all, ((0, -c_all.shape[0] % 8), (0, 0)))
    m3 = _modulation(c_all, mod_w[0], mod_b[0][None, :]).reshape(c_all.shape[0], N_MOD, d)

    lb = jnp.cumsum(jax.nn.softmax(hgrn_lb_logits.astype(F32), axis=0), axis=0)[0]
    wg, wu, wd = _to_bf16(ffn_w_gate[0]), _to_bf16(ffn_w_up[0]), _to_bf16(ffn_w_down[0])
    w_in_b = _to_bf16(w_in[0])

    h1 = _half_ffn(x, m3, wg, wu, wd, 0, mod_base=0, pos=_grid_pos_embed(n_lat))
    n_ctx = ctx.shape[1]
    slab = math.gcd(batch * n_ctx, FFN_TILE)
    hc1 = _half_ffn(ctx.reshape(-1, slab, d), m3, wg, wu, wd, 0, mod_base=0, mod_row=batch)

    vc, lffc, lfbc = (a.reshape(batch, n_ctx, a.shape[-1])
                      for a in _input_proj(hc1, m3, w_in_b, lb, mod_row=batch))
    v, lff, lfb, q, g, hy, sg = _input_proj(h1, m3, w_in_b, lb)
    o_a = _hgrn2(q, v, lff, lfb, g, vc, lffc, lfbc, hgrn_norm_w[0][None, :])

    taps, mass = _hyena_filters(n_lat, hyena_w1[0], hyena_b1[0], hyena_freq1[0], hyena_w2[0], hyena_b2[0],
                                hyena_freq2[0], hyena_w3[0])
    tw = _twiddles(n_lat)
    fwd, inv = _odd_dft_matrices(n_lat // TIME_SPLIT)
    ktab = _filter_spectrum(fwd, taps, mass, tw)
    o_b = _hyena(hy, hyena_conv_w[0], hyena_conv_b[0], hyena_bias[0], tw, fwd, inv, ktab)

    mixers = (o_a, o_b, sg, _to_bf16(w_proj_a[0]), _to_bf16(w_proj_b[0]), _to_bf16(w_out[0]))
    return _half_ffn(h1, m3, wg, wu, wd, 1, mod_base=6, mixers=mixers, final_norm_w=final_norm_w[None, :])
```

```python
import functools
import math

import jax
import jax.numpy as jnp
from jax import lax
from jax.experimental import pallas as pl
from jax.experimental.pallas import tpu as pltpu

F32 = jnp.float32
BF16 = jnp.bfloat16
HIGHEST = lax.Precision.HIGHEST

D_MODEL = 1024
GRID_W = 64
HEADS = 4
HEAD_DIM = 128
KW = HEADS * HEAD_DIM
HY_W = 512
D_FF = 2816
N_MOD = 9
RMS_EPS = 1e-6
HYENA_EMB = 33
HYENA_BANDS = (HYENA_EMB - 1) // 2
HYENA_FFN = 64
HYENA_FAST_DECAY = 0.3
HYENA_SLOW_DECAY = 1.5
HYENA_TARGET = 1e-2
HYENA_SHIFT = 0.05
HYENA_L1_EPS = 1e-6

COL_V = 0
COL_FFW = COL_V + KW
COL_FBW = COL_FFW + KW
COL_Q = COL_FBW + KW
COL_G = COL_Q + KW
COL_HY = COL_G + KW
COL_MERGE = COL_HY + 3 * HY_W
IN_COLS = COL_MERGE + 2 * D_MODEL

V7X_VMEM_BYTES = 64 * 1024 * 1024
VMEM_LIMIT = V7X_VMEM_BYTES - 8 * 1024 * 1024

TOKEN_TILE = 512
PROJ_TILE = 1024
FFN_TILE = 1024
FF_CHUNK = 256
SCAN_CHUNK = 128
SCAN_UNROLL = 2
FREQ_TILE = 256
TIME_SPLIT = 4
FILT_TILE = 256
FEAT_PAD = 128
LANES = 128


def _const_spec(shape):
    nd = len(shape)
    return pl.BlockSpec(shape, lambda *_: (0,) * nd, pipeline_mode=pl.Buffered(1))


def _params(n_grid):
    return pltpu.CompilerParams(dimension_semantics=("arbitrary",) * n_grid, vmem_limit_bytes=VMEM_LIMIT)


def _rms(x):
    return x * lax.rsqrt(jnp.mean(x * x, axis=-1, keepdims=True) + RMS_EPS)


def _norm_mod(h, shift, scale):
    return _rms(h) * (1.0 + scale) + shift


def _dot(a, b):
    return jnp.dot(a, b, preferred_element_type=F32)


CAST_BLOCK_BYTES = 4 * 1024 * 1024


def _cast_kernel(x_ref, o_ref):
    o_ref[...] = x_ref[...].astype(BF16)


def _to_bf16(w):
    w2 = w.reshape(-1, w.shape[-1])
    r, c = w2.shape
    tr = next(r // k for k in range(1, r + 1)
              if r % k == 0 and (r // k) % 8 == 0 and (r // k) * c * 4 <= CAST_BLOCK_BYTES)
    out = pl.pallas_call(
        _cast_kernel,
        out_shape=jax.ShapeDtypeStruct((r, c), BF16),
        grid=(r // tr,),
        in_specs=[pl.BlockSpec((tr, c), lambda i: (i, 0))],
        out_specs=pl.BlockSpec((tr, c), lambda i: (i, 0)),
        compiler_params=_params(1),
        name="cast",
    )(w2)
    return out.reshape(w.shape)


def _mod_kernel(c_ref, w_ref, b_ref, o_ref):
    c = c_ref[...]
    a = c * jax.nn.sigmoid(c)
    o_ref[...] = _dot(a.astype(BF16), w_ref[...].astype(BF16)) + b_ref[...]


def _modulation(c_all, mod_w, mod_b):
    rows = c_all.shape[0]
    tn = D_MODEL
    return pl.pallas_call(
        _mod_kernel,
        out_shape=jax.ShapeDtypeStruct((rows, N_MOD * D_MODEL), F32),
        grid=(N_MOD,),
        in_specs=[
            pl.BlockSpec((rows, D_MODEL), lambda j: (0, 0)),
            pl.BlockSpec((D_MODEL, tn), lambda j: (0, j)),
            pl.BlockSpec((1, tn), lambda j: (0, j)),
        ],
        out_specs=pl.BlockSpec((rows, tn), lambda j: (0, j)),
        compiler_params=_params(1),
        name="mod",
    )(c_all, mod_w, mod_b)


def _ffn_kernel(*refs, mod_base, add_pos, mixers, final_norm):
    refs = list(refs)
    h_ref = refs.pop(0)
    pos_ref = refs.pop(0) if add_pos else None
    m_ref = refs.pop(0)
    if mixers:
        oa_ref, ob_ref, sg_ref, wpa_ref, wpb_ref, wo_ref = refs[:6]
        refs = refs[6:]
    wg_ref, wu_ref, wd_ref = refs[:3]
    refs = refs[3:]
    fnw_ref = refs.pop(0) if final_norm else None
    o_ref = refs.pop(0)

    h = h_ref[0]
    if add_pos:
        h = h + pos_ref[...]
    if mixers:
        d = h.shape[1]
        ya = _dot(oa_ref[0], wpa_ref[...])
        yb = _dot(ob_ref[0], wpb_ref[...])
        y = sg_ref[0, :, :d].astype(F32) * ya + sg_ref[0, :, d:].astype(F32) * yb
        h = h + m_ref[0, 5:6, :] * _dot(y.astype(BF16), wo_ref[...])
    shift = m_ref[0, mod_base:mod_base + 1, :]
    scale = m_ref[0, mod_base + 1:mod_base + 2, :]
    gate = m_ref[0, mod_base + 2:mod_base + 3, :]
    nb = _norm_mod(h, shift, scale).astype(BF16)
    acc = jnp.zeros(h.shape, F32)
    for c in range(D_FF // FF_CHUNK):
        sl = slice(c * FF_CHUNK, (c + 1) * FF_CHUNK)
        g = _dot(nb, wg_ref[0, :, sl])
        u = _dot(nb, wu_ref[0, :, sl])
        a = (g * jax.nn.sigmoid(g) * u).astype(BF16)
        acc = acc + _dot(a, wd_ref[0, sl, :])
    out = h + 0.5 * gate * acc
    if final_norm:
        out = _rms(out) * fnw_ref[...]
    o_ref[0] = out


def _half_ffn(h, m3, wg, wu, wd, half, *, mod_base, mod_row=None, pos=None, mixers=None, final_norm_w=None):
    b, l, d = h.shape
    tm = min(TOKEN_TILE if mixers is not None else FFN_TILE, l)
    nt = l // tm
    row_map = (lambda j, i: (i, 0, 0)) if mod_row is None else (lambda j, i: (mod_row, 0, 0))
    tok = lambda w: pl.BlockSpec((1, tm, w), lambda j, i: (i, j, 0))
    in_specs = [tok(d)]
    args = [h]
    if pos is not None:
        in_specs.append(pl.BlockSpec((tm, d), lambda j, i: (j, 0)))
        args.append(pos)
    in_specs.append(pl.BlockSpec((1, N_MOD, d), row_map))
    args.append(m3)
    if mixers is not None:
        in_specs += [tok(a.shape[2]) for a in mixers[:3]] + [_const_spec(w.shape) for w in mixers[3:]]
        args += list(mixers)
    in_specs += [pl.BlockSpec((1,) + w.shape[1:], lambda j, i: (half, 0, 0), pipeline_mode=pl.Buffered(1))
                 for w in (wg, wu, wd)]
    args += [wg, wu, wd]
    if final_norm_w is not None:
        in_specs.append(_const_spec(final_norm_w.shape))
        args.append(final_norm_w)
    kern = functools.partial(_ffn_kernel, mod_base=mod_base, add_pos=pos is not None,
                             mixers=mixers is not None, final_norm=final_norm_w is not None)
    return pl.pallas_call(
        kern,
        out_shape=jax.ShapeDtypeStruct((b, l, d), F32),
        grid=(nt, b),
        in_specs=in_specs,
        out_specs=tok(d),
        compiler_params=_params(2),
        name="ffn_mix" if mixers is not None else "ffn",
    )(*args)


def _log2_forget(z, lb):
    return jnp.log2(lb + (1.0 - lb) * jax.nn.sigmoid(z))


def _proj_kernel(h_ref, m_ref, w_ref, lb_ref, *out_refs, full):
    h = h_ref[0]
    nb = _norm_mod(h, m_ref[0, 3:4, :], m_ref[0, 4:5, :]).astype(BF16)

    def proj(c0):
        return _dot(nb, w_ref[:, c0:c0 + KW])

    v_ref, lff_ref, lfb_ref = out_refs[:3]
    v_ref[0] = proj(COL_V).astype(BF16)
    lff_ref[0] = _log2_forget(proj(COL_FFW), lb_ref[0:1, :])
    lfb_ref[0] = _log2_forget(proj(COL_FBW), lb_ref[1:2, :])
    if full:
        q_ref, g_ref, hy_ref, mg_ref, par_ref = out_refs[3:]
        zq = proj(COL_Q)
        q_ref[0] = (zq * jax.nn.sigmoid(zq)).astype(BF16)
        zg = proj(COL_G)
        g_ref[0] = (zg * jax.nn.sigmoid(zg)).astype(BF16)
        part = par_ref.shape[1] // TIME_SPLIT
        for k in range(3 * HY_W // KW):
            z = proj(COL_HY + k * KW)
            for c in range(KW // LANES):
                par_ref[c] = z[:, c * LANES:(c + 1) * LANES]
                cols = slice(k * KW + c * LANES, k * KW + (c + 1) * LANES)
                for r in range(TIME_SPLIT):
                    hy_ref[0, r, :, cols] = par_ref[c, pl.ds(r, part, stride=TIME_SPLIT), :].astype(BF16)
        for k in range(2 * D_MODEL // KW):
            mg_ref[0, :, k * KW:(k + 1) * KW] = jax.nn.sigmoid(proj(COL_MERGE + k * KW)).astype(BF16)


def _input_proj(h, m3, w_in, lb, *, mod_row=None):
    b, l, d = h.shape
    full = mod_row is None
    tm = min(PROJ_TILE, l)
    nt = l // tm
    row_map = (lambda j, i: (i, 0, 0)) if full else (lambda j, i: (mod_row, 0, 0))
    ncols = IN_COLS if full else COL_Q
    tok = lambda w: pl.BlockSpec((1, tm, w), lambda j, i: (i, j, 0))
    shapes = [(KW, BF16), (KW, F32), (KW, F32)]
    if full:
        shapes += [(KW, BF16), (KW, BF16), None, (2 * D_MODEL, BF16)]
    out_shape = [jax.ShapeDtypeStruct((b, l, s[0]), s[1]) if s else
                 jax.ShapeDtypeStruct((b, TIME_SPLIT, l // TIME_SPLIT, 3 * HY_W), BF16) for s in shapes]
    out_specs = [tok(s[0]) if s else
                 pl.BlockSpec((1, TIME_SPLIT, tm // TIME_SPLIT, 3 * HY_W), lambda j, i: (i, 0, j, 0))
                 for s in shapes]
    return pl.pallas_call(
        functools.partial(_proj_kernel, full=full),
        out_shape=out_shape,
        grid=(nt, b),
        in_specs=[
            tok(d),
            pl.BlockSpec((1, N_MOD, d), row_map),
            pl.BlockSpec((d, ncols), lambda j, i: (0, 0), pipeline_mode=pl.Buffered(1)),
            _const_spec(lb.shape),
        ],
        out_specs=out_specs,
        scratch_shapes=[pltpu.VMEM((KW // LANES, tm, LANES), F32)] if full else [],
        compiler_params=_params(2),
        name="proj" if full else "proj_ctx",
    )(h, m3, w_in, lb)


def _split2(x):
    hi = x.astype(BF16)
    lo = (x - hi.astype(F32)).astype(BF16)
    return hi, lo


def _dot_nt(a, b):
    return lax.dot_general(a, b, (((1,), (1,)), ((), ())), preferred_element_type=F32)


def _scan_kernel(q_ref, v_ref, lff_ref, lfb_ref, g_ref, vc_ref, lffc_ref, lfbc_ref, nw_ref,
                 o_ref, oacc_ref, vt_ref, vtc_ref, st_ref):
    c = SCAN_CHUNK
    n_lat = q_ref.shape[1] // c
    n_ctx = vc_ref.shape[1] // c
    row = lax.broadcasted_iota(jnp.int32, (c, c), 0)
    col = lax.broadcasted_iota(jnp.int32, (c, c), 1)
    lower = row >= col
    upper = row <= col
    tri_l = jnp.where(lower, 1.0, 0.0).astype(BF16)
    tri_u = jnp.where(upper, 1.0, 0.0).astype(BF16)

    for j in range(n_lat):
        vt_ref[j] = v_ref[0, j * c:(j + 1) * c, :].astype(F32).T.astype(BF16)
    for j in range(n_ctx):
        vtc_ref[j] = vc_ref[0, j * c:(j + 1) * c, :].astype(F32).T.astype(BF16)

    heads = [slice(hd * HEAD_DIM, (hd + 1) * HEAD_DIM) for hd in range(HEADS)]
    half = c // 2

    def cumulate(direction, lf):
        hi, lo = _split2(lf)
        tri = tri_l if direction == 0 else tri_u
        return _dot(tri, hi) + _dot(tri, lo)

    def prepare(direction, lf, q, b=None):
        i1, i2, tot_i = (half // 2 - 1, half + half // 2 - 1, c - 1) if direction == 0 else \
            (half // 2, half + half // 2, 0)
        b = cumulate(direction, lf) if b is None else b
        k = 1.0 - jnp.exp2(lf)
        tot = b[tot_i:tot_i + 1, :]
        kd = (k * jnp.exp2(tot - b)).astype(BF16)
        dec = jnp.exp2(tot)
        if q is None:
            return kd, dec, None, None, None, None
        r1, r2 = b[i1:i1 + 1, :], b[i2:i2 + 1, :]
        ref = jnp.concatenate([jnp.broadcast_to(r1, (half, KW)), jnp.broadcast_to(r2, (half, KW))], axis=0)
        qf = q.astype(F32)
        qm = qf * jnp.exp2(b - ref)
        qd = (qf * jnp.exp2(b)).astype(BF16)
        km = (k * jnp.exp2(ref - b)).astype(BF16)
        if direction == 0:
            qx = qm[half:] * jnp.exp2(r2 - r1)
        else:
            qx = qm[:half] * jnp.exp2(r1 - r2)
        return kd, dec, qm.astype(BF16), qd, km, qx.astype(BF16)

    def advance(items):
        with_q = [p[2] is not None for _, p, _, _ in items]
        zero = jnp.zeros((half, HEAD_DIM), BF16)

        def pair_scores(d, qm, km, qx):
            q1, q2, k1, k2 = qm[:half], qm[half:], km[:half], km[half:]
            cat = lambda rows: jnp.concatenate([jnp.concatenate(r, axis=1) for r in rows], axis=0)
            if d == 0:
                return _dot_nt(cat([[q1, zero, zero], [zero, q2, qx]]), cat([[k1, zero, k1], [zero, k2, zero]]))
            return _dot_nt(cat([[q1, zero, qx], [zero, q2, zero]]), cat([[k1, zero, zero], [zero, k2, k2]]))

        scores = [[pair_scores(d, p[2][:, hs], p[4][:, hs], p[5][:, hs]) for hs in heads] if wq else None
                  for (d, p, _, _), wq in zip(items, with_q)]
        grow = [[_dot(vt[hs, :], p[0][:, hs]) for hs in heads] for _, p, vt, _ in items]
        state = {d: [st_ref[d, hd] for hd in range(HEADS)] for d in {d for d, *_ in items}}
        carry = []
        for n, (d, p, _, _) in enumerate(items):
            carry.append([_dot_nt(p[3][:, hs], state[d][hd].astype(BF16)) for hd, hs in enumerate(heads)]
                         if with_q[n] else None)
            state[d] = [state[d][hd] * p[1][:, hs] + grow[n][hd] for hd, hs in enumerate(heads)]
        for d, sts in state.items():
            for hd in range(HEADS):
                st_ref[d, hd] = sts[hd]
        outs = []
        for n, (d, _, _, v) in enumerate(items):
            if not with_q[n]:
                outs.append(None)
                continue
            mask = lower if d == 0 else upper
            o = [_dot(jnp.where(mask, scores[n][hd], 0.0).astype(BF16), v[:, hs]) + carry[n][hd]
                 for hd, hs in enumerate(heads)]
            outs.append(jnp.concatenate(o, axis=-1))
        return outs

    st_ref[...] = jnp.zeros(st_ref.shape, F32)
    advance([(0, prepare(0, lffc_ref[0, j * c:(j + 1) * c, :], None), vtc_ref[j], None) for j in range(n_ctx)]
            + [(1, prepare(1, lfbc_ref[0, j * c:(j + 1) * c, :], None), vtc_ref[j], None)
               for j in reversed(range(n_ctx))])

    nw = nw_ref[...]

    def finish(rows, o):
        normed = [_rms(o[:, hs]) * nw for hs in heads]
        o_ref[0, rows, :] = (jnp.concatenate(normed, axis=-1) * g_ref[0, rows, :].astype(F32)).astype(BF16)

    unroll = SCAN_UNROLL
    n_steps = n_lat // unroll

    def step_chunks(j):
        idx = [unroll * j + u for u in range(unroll)] + [n_lat - 1 - (unroll * j + u) for u in range(unroll)]
        start = (lambda i: i * c) if isinstance(j, int) else (lambda i: pl.multiple_of(i * c, c))
        return [(0 if n < unroll else 1, i, pl.ds(start(i), c)) for n, i in enumerate(idx)]

    def log_forget(d, r):
        return (lff_ref if d == 0 else lfb_ref)[0, r, :]

    def step(j, final):
        todo = step_chunks(j)
        cums = [cumulate(d, log_forget(d, r)) for d, _, r in todo]
        ops = [prepare(d, log_forget(d, r), q_ref[0, r, :], b) for (d, _, r), b in zip(todo, cums)]
        outs = advance([(d, p, vt_ref[i], v_ref[0, r, :]) for (d, i, r), p in zip(todo, ops)])
        for (_, _, r), o in zip(todo, outs):
            if final:
                finish(r, oacc_ref[r, :] + o)
            else:
                oacc_ref[r, :] = o

    def first_half(j, carry):
        step(j, False)
        return carry

    def second_half(j, carry):
        step(j, True)
        return carry

    lax.fori_loop(0, n_steps // 2, first_half, 0)
    lax.fori_loop(n_steps // 2, n_steps, second_half, 0)


def _hgrn2(q, v, lff, lfb, g, vc, lffc, lfbc, norm_w):
    b, l, _ = q.shape
    lc = vc.shape[1]
    c = SCAN_CHUNK
    seq = lambda n: pl.BlockSpec((1, n, KW), lambda i: (i, 0, 0))
    return pl.pallas_call(
        _scan_kernel,
        out_shape=jax.ShapeDtypeStruct((b, l, KW), BF16),
        grid=(b,),
        in_specs=[seq(l), seq(l), seq(l), seq(l), seq(l), seq(lc), seq(lc), seq(lc),
                  _const_spec(norm_w.shape)],
        out_specs=seq(l),
        scratch_shapes=[pltpu.VMEM((l, KW), F32), pltpu.VMEM((l // c, KW, c), BF16),
                        pltpu.VMEM((lc // c, KW, c), BF16), pltpu.VMEM((2, HEADS, HEAD_DIM, HEAD_DIM), F32)],
        compiler_params=_params(1),
        name="scan",
    )(q, v, lff, lfb, g, vc, lffc, lfbc, norm_w)


def _filter_kernel(z_ref, w1_ref, b1_ref, f1_ref, w2_ref, b2_ref, f2_ref, w3_ref, win_ref, h_ref, mass_ref):
    hp = dict(precision=HIGHEST, preferred_element_type=F32)
    h = jnp.sin(f1_ref[...] * (jnp.dot(z_ref[...], w1_ref[...], **hp) + b1_ref[...]))
    h = jnp.sin(f2_ref[...] * (jnp.dot(h, w2_ref[...], **hp) + b2_ref[...]))
    (h_hi, h_lo), (w_hi, w_lo) = _split2(h), _split2(w3_ref[...])
    h = _dot(h_hi, w_hi) + _dot(h_hi, w_lo) + _dot(h_lo, w_hi)
    win = win_ref[...]
    n_groups = h.shape[1] // HY_W
    h = jnp.concatenate([h[:, k * HY_W:(k + 1) * HY_W] * win for k in range(n_groups)], axis=-1)
    h_ref[...] = h.astype(BF16)

    @pl.when(pl.program_id(0) == 0)
    def _():
        mass_ref[...] = jnp.zeros(mass_ref.shape, F32)

    mass_ref[...] += jnp.sum(jnp.abs(h), axis=0, keepdims=True)


def _hyena_filters(l, w1, b1, fr1, w2, b2, fr2, w3):
    p = jnp.concatenate([jnp.arange(r, l, TIME_SPLIT) for r in range(TIME_SPLIT)]).astype(F32)
    t = p / (l - 1)
    w = 2.0 * math.pi * p / l
    f = jnp.linspace(1e-4, HYENA_BANDS - 1, HYENA_BANDS, dtype=F32)
    ang = w[:, None] * f[None, :]
    z = jnp.concatenate([t[:, None], jnp.cos(ang), -jnp.sin(ang)], axis=-1)
    max_decay = math.log(HYENA_TARGET) / HYENA_FAST_DECAY
    min_decay = math.log(HYENA_TARGET) / HYENA_SLOW_DECAY
    deltas = jnp.abs(jnp.linspace(min_decay, max_decay, HY_W, dtype=F32))
    window = jnp.exp(-t[:, None] * deltas[None, :]) + HYENA_SHIFT

    pad_c = lambda a, n: jnp.pad(a, ((0, 0), (0, n - a.shape[1])))
    pad_r = lambda a, n: jnp.pad(a, ((0, n - a.shape[0]), (0, 0)))
    z = pad_c(z, FEAT_PAD)
    w1p = pad_c(pad_r(w1, FEAT_PAD), FEAT_PAD)
    w2p = pad_c(pad_r(w2, FEAT_PAD), FEAT_PAD)
    w3p = pad_r(w3, FEAT_PAD)
    vec = lambda a: pad_c(a[None, :], FEAT_PAD)
    n_out = w3.shape[1]
    tl = FILT_TILE
    return pl.pallas_call(
        _filter_kernel,
        out_shape=[jax.ShapeDtypeStruct((l, n_out), BF16), jax.ShapeDtypeStruct((1, n_out), F32)],
        grid=(l // tl,),
        in_specs=[
            pl.BlockSpec((tl, FEAT_PAD), lambda i: (i, 0)),
            _const_spec(w1p.shape), _const_spec((1, FEAT_PAD)), _const_spec((1, FEAT_PAD)),
            _const_spec(w2p.shape), _const_spec((1, FEAT_PAD)), _const_spec((1, FEAT_PAD)),
            _const_spec(w3p.shape),
            pl.BlockSpec((tl, HY_W), lambda i: (i, 0)),
        ],
        out_specs=[pl.BlockSpec((tl, n_out), lambda i: (i, 0)), pl.BlockSpec((1, n_out), lambda i: (0, 0))],
        compiler_params=_params(1),
        name="filt",
    )(z, w1p, vec(b1), vec(fr1), w2p, vec(b2), vec(fr2), w3p, window)


def _dft_kernel(ar_ref, ai_ref, br_ref, bi_ref, cr_ref, ci_ref, dr_ref, di_ref, fwd_ref, inv_ref):
    tf = FREQ_TILE
    br, bi = br_ref[...], bi_ref[...]
    for t1 in range(fwd_ref.shape[1] // LANES):
        ar, ai = ar_ref[:, t1:t1 + 1], ai_ref[:, t1:t1 + 1]
        cols = slice(t1 * LANES, (t1 + 1) * LANES)
        fwd_ref[0:tf, cols] = (ar * br - ai * bi).astype(BF16)
        fwd_ref[tf:2 * tf, cols] = (ar * bi + ai * br).astype(BF16)
    dr, di = dr_ref[...], di_ref[...]
    for f1 in range(tf // LANES):
        cr, ci = cr_ref[0, :, f1:f1 + 1], ci_ref[0, :, f1:f1 + 1]
        inv_ref[0, :, f1 * LANES:(f1 + 1) * LANES] = (cr * dr - ci * di).astype(BF16)
        inv_ref[0, :, tf + f1 * LANES:tf + (f1 + 1) * LANES] = (cr * di + ci * dr).astype(BF16)


def _odd_dft_matrices(l):
    n_ang = 4 * l
    theta = 2.0 * math.pi / n_ang
    tf = FREQ_TILE
    nf = l // tf
    n_hi = l // LANES

    def cis(idx):
        a = (idx % n_ang).astype(F32) * theta
        return jnp.cos(a), jnp.sin(a)

    idx = jnp.arange(l, dtype=jnp.int32)
    odd = 2 * idx + 1
    lane = jnp.arange(LANES, dtype=jnp.int32)
    ar, ai = cis(odd[:, None] * (LANES * jnp.arange(n_hi, dtype=jnp.int32))[None, :])
    br, bi = cis(odd[:, None] * lane[None, :])
    g = jnp.arange(n_hi, dtype=jnp.int32).reshape(nf, 1, tf // LANES)
    cr, ci = cis(idx[None, :, None] * (2 * LANES * g))
    dr, di = cis(idx[:, None] * (2 * lane + 1)[None, :])
    row_tab = lambda w: pl.BlockSpec((tf, w), lambda i: (i, 0))
    return pl.pallas_call(
        _dft_kernel,
        out_shape=[jax.ShapeDtypeStruct((2 * l, l), BF16), jax.ShapeDtypeStruct((nf, l, 2 * tf), BF16)],
        grid=(nf,),
        in_specs=[row_tab(n_hi), row_tab(n_hi), row_tab(LANES), row_tab(LANES),
                  pl.BlockSpec((1, l, tf // LANES), lambda i: (i, 0, 0)),
                  pl.BlockSpec((1, l, tf // LANES), lambda i: (i, 0, 0)),
                  _const_spec((l, LANES)), _const_spec((l, LANES))],
        out_specs=[pl.BlockSpec((2 * tf, l), lambda i: (i, 0)), pl.BlockSpec((1, l, 2 * tf), lambda i: (i, 0, 0))],
        compiler_params=_params(1),
        name="dft",
    )(ar, ai, br, bi, cr, ci, dr, di)


def _butterfly(ac, as_, bc, bs, cph, sph):
    tc = cph * bc - sph * bs
    ts = cph * bs + sph * bc
    return ac + tc, as_ + ts, ac - tc, ts - as_


def _inv_butterfly(yc, ys, yhc, yhs, cph, sph):
    dc, ds = yc - yhc, ys + yhs
    return yc + yhc, ys - yhs, dc * cph + ds * sph, ds * cph - dc * sph


def _split4_forward(fc, fs, xs, tw):
    c1, s1, c1g, s1g, c2, s2 = (tw[:, k:k + 1] for k in range(6))
    p = [(_dot(fc, x).astype(tw.dtype), _dot(fs, x).astype(tw.dtype)) for x in xs]
    ev = _butterfly(*p[0], *p[2], c2, s2)
    od = _butterfly(*p[1], *p[3], c2, s2)
    return _butterfly(*ev[:2], *od[:2], c1, s1) + _butterfly(*ev[2:], *od[2:], c1g, s1g)


def _split4_inverse(y, tw):
    c1, s1, c1g, s1g, c2, s2 = (tw[:, k:k + 1] for k in range(6))
    at_f = _inv_butterfly(*y[0:4], c1, s1)
    at_g = _inv_butterfly(*y[4:8], c1g, s1g)
    r0c, r0s, r2c, r2s = _inv_butterfly(*at_f[:2], *at_g[:2], c2, s2)
    r1c, r1s, r3c, r3s = _inv_butterfly(*at_f[2:], *at_g[2:], c2, s2)
    return [(r0c, r0s), (r1c, r1s), (r2c, r2s), (r3c, r3s)]


def _kdft_kernel(fwd_ref, t0_ref, t1_ref, t2_ref, t3_ref, mass_ref, tw_ref, o_ref):
    tf = FREQ_TILE
    taps = (t0_ref, t1_ref, t2_ref, t3_ref)
    l = TIME_SPLIT * t0_ref.shape[0]
    fc, fs = fwd_ref[0:tf, :], fwd_ref[tf:2 * tf, :]
    tw = tw_ref[...]
    f = pl.program_id(0) * tf + lax.broadcasted_iota(jnp.int32, (tf, 1), 0)
    sgn_f = jnp.where(f % 2 == 0, 1.0, -1.0)
    signs = (sgn_f, -sgn_f, -sgn_f, sgn_f)
    for order in range(2):
        fw = slice((2 * order) * HY_W, (2 * order + 1) * HY_W)
        bw = slice((2 * order + 1) * HY_W, (2 * order + 2) * HY_W)
        a = _split4_forward(fc, fs, [t[:, fw] for t in taps], tw)
        h = _split4_forward(fc, fs, [t[:, bw] for t in taps], tw)
        scale = (1.0 / l) / (mass_ref[:, fw] + mass_ref[:, bw] + HYENA_L1_EPS)
        cols = slice(order * HY_W, (order + 1) * HY_W)
        for k in range(4):
            ac, as_, hc, hs = a[2 * k], a[2 * k + 1], h[2 * k], h[2 * k + 1]
            ec, es = tw[:, 6 + 2 * k:7 + 2 * k], tw[:, 7 + 2 * k:8 + 2 * k]
            o_ref[0, 2 * k, :, cols] = ((ac + signs[k] * (es * hc - ec * hs)) * scale).astype(BF16)
            o_ref[0, 2 * k + 1, :, cols] = ((-as_ + signs[k] * (ec * hc + es * hs)) * scale).astype(BF16)


def _twiddles(l):
    n_ang = 4 * l
    theta = 2.0 * math.pi / n_ang
    f = jnp.arange(l // 4, dtype=jnp.int32)
    g = l // 2 - 1 - f
    odd = lambda x: 2 * x + 1
    angles = [odd(f), odd(g), 2 * odd(f)] + [odd(x) * (l - 1) for x in (f, l - 1 - f, g, l - 1 - g)]
    cols = []
    for a in angles:
        r = (a % n_ang).astype(F32) * theta
        cols += [jnp.cos(r), jnp.sin(r)]
    cols += [jnp.zeros_like(cols[0])] * 2
    return jnp.stack(cols, axis=-1)


def _filter_spectrum(fwd, taps, mass, tw):
    l = taps.shape[0]
    m = l // TIME_SPLIT
    tf = FREQ_TILE
    tap_block = lambda r: pl.BlockSpec((m, taps.shape[1]), lambda i: (r, 0), pipeline_mode=pl.Buffered(1))
    return pl.pallas_call(
        _kdft_kernel,
        out_shape=jax.ShapeDtypeStruct((m // tf, 8, tf, 2 * HY_W), BF16),
        grid=(m // tf,),
        in_specs=[pl.BlockSpec((2 * tf, m), lambda i: (i, 0))] + [tap_block(r) for r in range(TIME_SPLIT)]
        + [_const_spec(mass.shape), pl.BlockSpec((tf, tw.shape[1]), lambda i: (i, 0))],
        out_specs=pl.BlockSpec((1, 8, tf, 2 * HY_W), lambda i: (i, 0, 0, 0)),
        compiler_params=_params(1),
        name="kdft",
    )(fwd, taps, taps, taps, taps, mass, tw)


HYENA_SUB = 128


def _short_conv_split(x_ref, cw, cb):
    x = [x_ref[0, r].astype(F32) for r in range(TIME_SPLIT)]
    m = x[0].shape[0]
    t = lax.broadcasted_iota(jnp.int32, (m, 1), 0)
    before = jnp.where(t == 0, 0.0, pltpu.roll(x[-1], 1, 0))
    after = jnp.where(t == m - 1, 0.0, pltpu.roll(x[0], m - 1, 0))
    prev = [before] + x[:-1]
    nxt = x[1:] + [after]
    return [cb + prev[r] * cw[0:1, :] + x[r] * cw[1:2, :] + nxt[r] * cw[2:3, :] for r in range(TIME_SPLIT)]


def _hyena_kernel(hv_ref, hx_ref, cw_ref, cb_ref, hb_ref, tw_ref, fwd_ref, inv_ref, kt_ref,
                  o_ref, u_ref, acc_ref, y_ref, nat_ref):
    order = pl.program_id(1)
    i = pl.program_id(2)
    tf = FREQ_TILE
    m, wide = u_ref.shape
    w = wide // TIME_SPLIT
    last = m // tf - 1
    lanes = [slice(r * w, (r + 1) * w) for r in range(TIME_SPLIT)]

    def restart(parts, bias):
        for r, part in enumerate(parts):
            u_ref[:, lanes[r]] = part.astype(BF16)
            acc_ref[:, lanes[r]] = bias * part

    @pl.when((order == 0) & (i == 0))
    def _():
        restart(_short_conv_split(hv_ref, cw_ref[0], cb_ref[0]), hb_ref[0:1, :])

    xs = [u_ref[:, ln] for ln in lanes]
    for sb in range(tf // HYENA_SUB):
        rc = slice(sb * HYENA_SUB, (sb + 1) * HYENA_SUB)
        rs = slice(tf + sb * HYENA_SUB, tf + (sb + 1) * HYENA_SUB)
        tw = tw_ref[rc, :].astype(BF16)
        x = _split4_forward(fwd_ref[rc, :], fwd_ref[rs, :], xs, tw)
        y = []
        for k in range(4):
            kr, ki = kt_ref[0, 2 * k, rc, :], kt_ref[0, 2 * k + 1, rc, :]
            y += [kr * x[2 * k] + ki * x[2 * k + 1], kr * x[2 * k + 1] - ki * x[2 * k]]
        for r, (yc, ys) in enumerate(_split4_inverse(y, tw)):
            y_ref[rc, lanes[r]] = yc
            y_ref[rs, lanes[r]] = ys
    acc_ref[...] += _dot(inv_ref[0], y_ref[...])

    @pl.when((order == 0) & (i == last))
    def _():
        gates = _short_conv_split(hx_ref, cw_ref[1], cb_ref[1])
        restart([g * acc_ref[:, ln] for g, ln in zip(gates, lanes)], hb_ref[1:2, :])

    @pl.when((order == 1) & (i == last))
    def _():
        gates = _short_conv_split(hx_ref, cw_ref[2], cb_ref[2])
        z = [g * acc_ref[:, ln] for g, ln in zip(gates, lanes)]
        for k in range(w // LANES):
            cols = slice(k * LANES, (k + 1) * LANES)
            for r in range(TIME_SPLIT):
                nat_ref[k, pl.ds(r, m, stride=TIME_SPLIT), :] = z[r][:, cols]
            o_ref[0, :, cols] = nat_ref[k].astype(BF16)


def _hyena(hy, conv_w, conv_b, hy_bias, tw, fwd, inv, ktab):
    b, _, m, _ = hy.shape
    tf = FREQ_TILE
    nf = m // tf
    cw = conv_w.reshape(conv_w.shape[0], 3, HY_W).transpose(1, 0, 2)
    cb = conv_b.reshape(3, 1, HY_W)
    return pl.pallas_call(
        _hyena_kernel,
        out_shape=jax.ShapeDtypeStruct((b, TIME_SPLIT * m, HY_W), BF16),
        grid=(b, 2, nf),
        in_specs=[
            pl.BlockSpec((1, TIME_SPLIT, m, HY_W), lambda bi, o, i: (bi, 0, 0, 0)),
            pl.BlockSpec((1, TIME_SPLIT, m, HY_W), lambda bi, o, i: (bi, 0, 0, 1 + o)),
            _const_spec(cw.shape), _const_spec(cb.shape), _const_spec(hy_bias.shape),
            pl.BlockSpec((tf, tw.shape[1]), lambda bi, o, i: (i, 0)),
            pl.BlockSpec((2 * tf, m), lambda bi, o, i: (i, 0)),
            pl.BlockSpec((1, m, 2 * tf), lambda bi, o, i: (i, 0, 0)),
            pl.BlockSpec((1, 8, tf, HY_W), lambda bi, o, i: (i, 0, 0, o)),
        ],
        out_specs=pl.BlockSpec((1, TIME_SPLIT * m, HY_W), lambda bi, o, i: (bi, 0, 0)),
        scratch_shapes=[pltpu.VMEM((m, TIME_SPLIT * HY_W), BF16), pltpu.VMEM((m, TIME_SPLIT * HY_W), F32),
                        pltpu.VMEM((2 * tf, TIME_SPLIT * HY_W), BF16),
                        pltpu.VMEM((HY_W // LANES, TIME_SPLIT * m, LANES), F32)],
        compiler_params=_params(3),
        name="hyena",
    )(hy, hy, cw, cb, hy_bias, tw, fwd, inv, ktab)


def _grid_pos_embed(n_tokens):
    rows = n_tokens // GRID_W
    quarter = D_MODEL // 4
    omega = 1.0 / (10000.0 ** (jnp.arange(quarter, dtype=F32) / quarter))
    ar = jnp.arange(rows, dtype=F32)[:, None] * omega
    ac = jnp.arange(GRID_W, dtype=F32)[:, None] * omega
    er = jnp.concatenate([jnp.sin(ar), jnp.cos(ar)], axis=-1)
    ec = jnp.concatenate([jnp.sin(ac), jnp.cos(ac)], axis=-1)
    emb = jnp.concatenate([jnp.broadcast_to(er[:, None, :], (rows, GRID_W, D_MODEL // 2)),
                           jnp.broadcast_to(ec[None, :, :], (rows, GRID_W, D_MODEL // 2))], axis=-1)
    return emb.reshape(rows * GRID_W, D_MODEL)


def kernel(x, c, ctx, c_ctx, mod_w, mod_b, ffn_w_gate, ffn_w_up, ffn_w_down, w_in, hgrn_lb_logits,
           hgrn_norm_w, hyena_conv_w, hyena_conv_b, hyena_w1, hyena_b1, hyena_freq1, hyena_w2, hyena_b2,
           hyena_freq2, hyena_w3, hyena_bias, w_proj_a, w_proj_b, w_out, final_norm_w):
    assert mod_w.shape[0] == 1, "single-layer configuration"
    batch, n_lat, d = x.shape

    c_all = jnp.concatenate([c, c_ctx[None, :]], axis=0)
    c_all = jnp.pad(c_all, ((0, -c_all.shape[0] % 8), (0, 0)))
    m3 = _modulation(c_all, mod_w[0], mod_b[0][None, :]).reshape(c_all.shape[0], N_MOD, d)

    lb = jnp.cumsum(jax.nn.softmax(hgrn_lb_logits.astype(F32), axis=0), axis=0)[0]
    wg, wu, wd = _to_bf16(ffn_w_gate[0]), _to_bf16(ffn_w_up[0]), _to_bf16(ffn_w_down[0])
    w_in_b = _to_bf16(w_in[0])

    h1 = _half_ffn(x, m3, wg, wu, wd, 0, mod_base=0, pos=_grid_pos_embed(n_lat))
    n_ctx = ctx.shape[1]
    slab = math.gcd(batch * n_ctx, FFN_TILE)
    hc1 = _half_ffn(ctx.reshape(-1, slab, d), m3, wg, wu, wd, 0, mod_base=0, mod_row=batch)

    vc, lffc, lfbc = (a.reshape(batch, n_ctx, a.shape[-1])
                      for a in _input_proj(hc1, m3, w_in_b, lb, mod_row=batch))
    v, lff, lfb, q, g, hy, sg = _input_proj(h1, m3, w_in_b, lb)
    o_a = _hgrn2(q, v, lff, lfb, g, vc, lffc, lfbc, hgrn_norm_w[0][None, :])

    taps, mass = _hyena_filters(n_lat, hyena_w1[0], hyena_b1[0], hyena_freq1[0], hyena_w2[0], hyena_b2[0],
                                hyena_freq2[0], hyena_w3[0])
    tw = _twiddles(n_lat)
    fwd, inv = _odd_dft_matrices(n_lat // TIME_SPLIT)
    ktab = _filter_spectrum(fwd, taps, mass, tw)
    o_b = _hyena(hy, hyena_conv_w[0], hyena_conv_b[0], hyena_bias[0], tw, fwd, inv, ktab)

    mixers = (o_a, o_b, sg, _to_bf16(w_proj_a[0]), _to_bf16(w_proj_b[0]), _to_bf16(w_out[0]))
    return _half_ffn(h1, m3, wg, wu, wd, 1, mod_base=6, mixers=mixers, final_norm_w=final_norm_w[None, :])
```

```python
import functools
import math

import jax
import jax.numpy as jnp
from jax import lax
from jax.experimental import pallas as pl
from jax.experimental.pallas import tpu as pltpu

F32 = jnp.float32
BF16 = jnp.bfloat16
HIGHEST = lax.Precision.HIGHEST

D_MODEL = 1024
GRID_W = 64
HEADS = 4
HEAD_DIM = 128
KW = HEADS * HEAD_DIM
HY_W = 512
D_FF = 2816
N_MOD = 9
RMS_EPS = 1e-6
HYENA_EMB = 33
HYENA_BANDS = (HYENA_EMB - 1) // 2
HYENA_FFN = 64
HYENA_FAST_DECAY = 0.3
HYENA_SLOW_DECAY = 1.5
HYENA_TARGET = 1e-2
HYENA_SHIFT = 0.05
HYENA_L1_EPS = 1e-6

COL_V = 0
COL_FFW = COL_V + KW
COL_FBW = COL_FFW + KW
COL_Q = COL_FBW + KW
COL_G = COL_Q + KW
COL_HY = COL_G + KW
COL_MERGE = COL_HY + 3 * HY_W
IN_COLS = COL_MERGE + 2 * D_MODEL

V7X_VMEM_BYTES = 64 * 1024 * 1024
VMEM_LIMIT = V7X_VMEM_BYTES - 8 * 1024 * 1024

TOKEN_TILE = 512
PROJ_TILE = 1024
FFN_TILE = 1024
FF_CHUNK = 256
SCAN_CHUNK = 128
SCAN_UNROLL = 2
FREQ_TILE = 512
TIME_SPLIT = 4
FILT_TILE = 256
FEAT_PAD = 128
LANES = 128


def _const_spec(shape):
    nd = len(shape)
    return pl.BlockSpec(shape, lambda *_: (0,) * nd, pipeline_mode=pl.Buffered(1))


def _params(n_grid):
    return pltpu.CompilerParams(dimension_semantics=("arbitrary",) * n_grid, vmem_limit_bytes=VMEM_LIMIT)


def _rms(x):
    return x * lax.rsqrt(jnp.mean(x * x, axis=-1, keepdims=True) + RMS_EPS)


def _norm_mod(h, shift, scale):
    return _rms(h) * (1.0 + scale) + shift


def _dot(a, b):
    return jnp.dot(a, b, preferred_element_type=F32)


CAST_BLOCK_BYTES = 4 * 1024 * 1024


def _cast_kernel(x_ref, o_ref):
    o_ref[...] = x_ref[...].astype(BF16)


def _to_bf16(w):
    w2 = w.reshape(-1, w.shape[-1])
    r, c = w2.shape
    tr = next(r // k for k in range(1, r + 1)
              if r % k == 0 and (r // k) % 8 == 0 and (r // k) * c * 4 <= CAST_BLOCK_BYTES)
    out = pl.pallas_call(
        _cast_kernel,
        out_shape=jax.ShapeDtypeStruct((r, c), BF16),
        grid=(r // tr,),
        in_specs=[pl.BlockSpec((tr, c), lambda i: (i, 0))],
        out_specs=pl.BlockSpec((tr, c), lambda i: (i, 0)),
        compiler_params=_params(1),
        name="cast",
    )(w2)
    return out.reshape(w.shape)


def _mod_kernel(c_ref, w_ref, b_ref, o_ref):
    c = c_ref[...]
    a = c * jax.nn.sigmoid(c)
    o_ref[...] = _dot(a.astype(BF16), w_ref[...].astype(BF16)) + b_ref[...]


def _modulation(c_all, mod_w, mod_b):
    rows = c_all.shape[0]
    tn = D_MODEL
    return pl.pallas_call(
        _mod_kernel,
        out_shape=jax.ShapeDtypeStruct((rows, N_MOD * D_MODEL), F32),
        grid=(N_MOD,),
        in_specs=[
            pl.BlockSpec((rows, D_MODEL), lambda j: (0, 0)),
            pl.BlockSpec((D_MODEL, tn), lambda j: (0, j)),
            pl.BlockSpec((1, tn), lambda j: (0, j)),
        ],
        out_specs=pl.BlockSpec((rows, tn), lambda j: (0, j)),
        compiler_params=_params(1),
        name="mod",
    )(c_all, mod_w, mod_b)


def _ffn_kernel(*refs, mod_base, add_pos, mixers, final_norm):
    refs = list(refs)
    h_ref = refs.pop(0)
    pos_ref = refs.pop(0) if add_pos else None
    m_ref = refs.pop(0)
    if mixers:
        oa_ref, ob_ref, sg_ref, wpa_ref, wpb_ref, wo_ref = refs[:6]
        refs = refs[6:]
    wg_ref, wu_ref, wd_ref = refs[:3]
    refs = refs[3:]
    fnw_ref = refs.pop(0) if final_norm else None
    o_ref = refs.pop(0)

    h = h_ref[0]
    if add_pos:
        h = h + pos_ref[...]
    if mixers:
        d = h.shape[1]
        ya = _dot(oa_ref[0], wpa_ref[...])
        yb = _dot(ob_ref[0], wpb_ref[...])
        y = sg_ref[0, :, :d].astype(F32) * ya + sg_ref[0, :, d:].astype(F32) * yb
        h = h + m_ref[0, 5:6, :] * _dot(y.astype(BF16), wo_ref[...])
    shift = m_ref[0, mod_base:mod_base + 1, :]
    scale = m_ref[0, mod_base + 1:mod_base + 2, :]
    gate = m_ref[0, mod_base + 2:mod_base + 3, :]
    nb = _norm_mod(h, shift, scale).astype(BF16)
    acc = jnp.zeros(h.shape, F32)
    for c in range(D_FF // FF_CHUNK):
        sl = slice(c * FF_CHUNK, (c + 1) * FF_CHUNK)
        g = _dot(nb, wg_ref[0, :, sl])
        u = _dot(nb, wu_ref[0, :, sl])
        a = (g * jax.nn.sigmoid(g) * u).astype(BF16)
        acc = acc + _dot(a, wd_ref[0, sl, :])
    out = h + 0.5 * gate * acc
    if final_norm:
        out = _rms(out) * fnw_ref[...]
    o_ref[0] = out


def _half_ffn(h, m3, wg, wu, wd, half, *, mod_base, mod_row=None, pos=None, mixers=None, final_norm_w=None):
    b, l, d = h.shape
    tm = min(TOKEN_TILE if mixers is not None else FFN_TILE, l)
    nt = l // tm
    row_map = (lambda j, i: (i, 0, 0)) if mod_row is None else (lambda j, i: (mod_row, 0, 0))
    tok = lambda w: pl.BlockSpec((1, tm, w), lambda j, i: (i, j, 0))
    in_specs = [tok(d)]
    args = [h]
    if pos is not None:
        in_specs.append(pl.BlockSpec((tm, d), lambda j, i: (j, 0)))
        args.append(pos)
    in_specs.append(pl.BlockSpec((1, N_MOD, d), row_map))
    args.append(m3)
    if mixers is not None:
        in_specs += [tok(a.shape[2]) for a in mixers[:3]] + [_const_spec(w.shape) for w in mixers[3:]]
        args += list(mixers)
    in_specs += [pl.BlockSpec((1,) + w.shape[1:], lambda j, i: (half, 0, 0), pipeline_mode=pl.Buffered(1))
                 for w in (wg, wu, wd)]
    args += [wg, wu, wd]
    if final_norm_w is not None:
        in_specs.append(_const_spec(final_norm_w.shape))
        args.append(final_norm_w)
    kern = functools.partial(_ffn_kernel, mod_base=mod_base, add_pos=pos is not None,
                             mixers=mixers is not None, final_norm=final_norm_w is not None)
    return pl.pallas_call(
        kern,
        out_shape=jax.ShapeDtypeStruct((b, l, d), F32),
        grid=(nt, b),
        in_specs=in_specs,
        out_specs=tok(d),
        compiler_params=_params(2),
        name="ffn_mix" if mixers is not None else "ffn",
    )(*args)


def _log2_forget(z, lb):
    return jnp.log2(lb + (1.0 - lb) * jax.nn.sigmoid(z))


def _proj_kernel(h_ref, m_ref, w_ref, lb_ref, *out_refs, full):
    h = h_ref[0]
    nb = _norm_mod(h, m_ref[0, 3:4, :], m_ref[0, 4:5, :]).astype(BF16)

    def proj(c0):
        return _dot(nb, w_ref[:, c0:c0 + KW])

    v_ref, lff_ref, lfb_ref = out_refs[:3]
    v_ref[0] = proj(COL_V).astype(BF16)
    lff_ref[0] = _log2_forget(proj(COL_FFW), lb_ref[0:1, :])
    lfb_ref[0] = _log2_forget(proj(COL_FBW), lb_ref[1:2, :])
    if full:
        q_ref, g_ref, hy_ref, mg_ref, par_ref = out_refs[3:]
        zq = proj(COL_Q)
        q_ref[0] = (zq * jax.nn.sigmoid(zq)).astype(BF16)
        zg = proj(COL_G)
        g_ref[0] = (zg * jax.nn.sigmoid(zg)).astype(BF16)
        part = par_ref.shape[1] // TIME_SPLIT
        for k in range(3 * HY_W // KW):
            z = proj(COL_HY + k * KW)
            for c in range(KW // LANES):
                par_ref[c] = z[:, c * LANES:(c + 1) * LANES]
                cols = slice(k * KW + c * LANES, k * KW + (c + 1) * LANES)
                for r in range(TIME_SPLIT):
                    hy_ref[0, r, :, cols] = par_ref[c, pl.ds(r, part, stride=TIME_SPLIT), :].astype(BF16)
        for k in range(2 * D_MODEL // KW):
            mg_ref[0, :, k * KW:(k + 1) * KW] = jax.nn.sigmoid(proj(COL_MERGE + k * KW)).astype(BF16)


def _input_proj(h, m3, w_in, lb, *, mod_row=None):
    b, l, d = h.shape
    full = mod_row is None
    tm = min(PROJ_TILE, l)
    nt = l // tm
    row_map = (lambda j, i: (i, 0, 0)) if full else (lambda j, i: (mod_row, 0, 0))
    ncols = IN_COLS if full else COL_Q
    tok = lambda w: pl.BlockSpec((1, tm, w), lambda j, i: (i, j, 0))
    shapes = [(KW, BF16), (KW, F32), (KW, F32)]
    if full:
        shapes += [(KW, BF16), (KW, BF16), None, (2 * D_MODEL, BF16)]
    out_shape = [jax.ShapeDtypeStruct((b, l, s[0]), s[1]) if s else
                 jax.ShapeDtypeStruct((b, TIME_SPLIT, l // TIME_SPLIT, 3 * HY_W), BF16) for s in shapes]
    out_specs = [tok(s[0]) if s else
                 pl.BlockSpec((1, TIME_SPLIT, tm // TIME_SPLIT, 3 * HY_W), lambda j, i: (i, 0, j, 0))
                 for s in shapes]
    return pl.pallas_call(
        functools.partial(_proj_kernel, full=full),
        out_shape=out_shape,
        grid=(nt, b),
        in_specs=[
            tok(d),
            pl.BlockSpec((1, N_MOD, d), row_map),
            pl.BlockSpec((d, ncols), lambda j, i: (0, 0), pipeline_mode=pl.Buffered(1)),
            _const_spec(lb.shape),
        ],
        out_specs=out_specs,
        scratch_shapes=[pltpu.VMEM((KW // LANES, tm, LANES), F32)] if full else [],
        compiler_params=_params(2),
        name="proj" if full else "proj_ctx",
    )(h, m3, w_in, lb)


def _split2(x):
    hi = x.astype(BF16)
    lo = (x - hi.astype(F32)).astype(BF16)
    return hi, lo


def _dot_nt(a, b):
    return lax.dot_general(a, b, (((1,), (1,)), ((), ())), preferred_element_type=F32)


def _scan_kernel(q_ref, v_ref, lff_ref, lfb_ref, g_ref, vc_ref, lffc_ref, lfbc_ref, nw_ref,
                 o_ref, oacc_ref, vt_ref, vtc_ref, st_ref):
    c = SCAN_CHUNK
    n_lat = q_ref.shape[1] // c
    n_ctx = vc_ref.shape[1] // c
    row = lax.broadcasted_iota(jnp.int32, (c, c), 0)
    col = lax.broadcasted_iota(jnp.int32, (c, c), 1)
    lower = row >= col
    upper = row <= col
    tri_l = jnp.where(lower, 1.0, 0.0).astype(BF16)
    tri_u = jnp.where(upper, 1.0, 0.0).astype(BF16)

    for j in range(n_lat):
        vt_ref[j] = v_ref[0, j * c:(j + 1) * c, :].astype(F32).T.astype(BF16)
    for j in range(n_ctx):
        vtc_ref[j] = vc_ref[0, j * c:(j + 1) * c, :].astype(F32).T.astype(BF16)

    heads = [slice(hd * HEAD_DIM, (hd + 1) * HEAD_DIM) for hd in range(HEADS)]
    half = c // 2

    def cumulate(direction, lf):
        hi, lo = _split2(lf)
        tri = tri_l if direction == 0 else tri_u
        return _dot(tri, hi) + _dot(tri, lo)

    def prepare(direction, lf, q, b=None):
        i1, i2, tot_i = (half // 2 - 1, half + half // 2 - 1, c - 1) if direction == 0 else \
            (half // 2, half + half // 2, 0)
        b = cumulate(direction, lf) if b is None else b
        k = 1.0 - jnp.exp2(lf)
        tot = b[tot_i:tot_i + 1, :]
        kd = (k * jnp.exp2(tot - b)).astype(BF16)
        dec = jnp.exp2(tot)
        if q is None:
            return kd, dec, None, None, None, None
        r1, r2 = b[i1:i1 + 1, :], b[i2:i2 + 1, :]
        ref = jnp.concatenate([jnp.broadcast_to(r1, (half, KW)), jnp.broadcast_to(r2, (half, KW))], axis=0)
        qf = q.astype(F32)
        qm = qf * jnp.exp2(b - ref)
        qd = (qf * jnp.exp2(b)).astype(BF16)
        km = (k * jnp.exp2(ref - b)).astype(BF16)
        if direction == 0:
            qx = qm[half:] * jnp.exp2(r2 - r1)
        else:
            qx = qm[:half] * jnp.exp2(r1 - r2)
        return kd, dec, qm.astype(BF16), qd, km, qx.astype(BF16)

    def advance(items):
        with_q = [p[2] is not None for _, p, _, _ in items]
        zero = jnp.zeros((half, HEAD_DIM), BF16)

        def pair_scores(d, qm, km, qx):
            q1, q2, k1, k2 = qm[:half], qm[half:], km[:half], km[half:]
            cat = lambda rows: jnp.concatenate([jnp.concatenate(r, axis=1) for r in rows], axis=0)
            if d == 0:
                return _dot_nt(cat([[q1, zero, zero], [zero, q2, qx]]), cat([[k1, zero, k1], [zero, k2, zero]]))
            return _dot_nt(cat([[q1, zero, qx], [zero, q2, zero]]), cat([[k1, zero, zero], [zero, k2, k2]]))

        scores = [[pair_scores(d, p[2][:, hs], p[4][:, hs], p[5][:, hs]) for hs in heads] if wq else None
                  for (d, p, _, _), wq in zip(items, with_q)]
        grow = [[_dot(vt[hs, :], p[0][:, hs]) for hs in heads] for _, p, vt, _ in items]
        state = {d: [st_ref[d, hd] for hd in range(HEADS)] for d in {d for d, *_ in items}}
        carry = []
        for n, (d, p, _, _) in enumerate(items):
            carry.append([_dot_nt(p[3][:, hs], state[d][hd].astype(BF16)) for hd, hs in enumerate(heads)]
                         if with_q[n] else None)
            state[d] = [state[d][hd] * p[1][:, hs] + grow[n][hd] for hd, hs in enumerate(heads)]
        for d, sts in state.items():
            for hd in range(HEADS):
                st_ref[d, hd] = sts[hd]
        outs = []
        for n, (d, _, _, v) in enumerate(items):
            if not with_q[n]:
                outs.append(None)
                continue
            mask = lower if d == 0 else upper
            o = [_dot(jnp.where(mask, scores[n][hd], 0.0).astype(BF16), v[:, hs]) + carry[n][hd]
                 for hd, hs in enumerate(heads)]
            outs.append(jnp.concatenate(o, axis=-1))
        return outs

    st_ref[...] = jnp.zeros(st_ref.shape, F32)
    advance([(0, prepare(0, lffc_ref[0, j * c:(j + 1) * c, :], None), vtc_ref[j], None) for j in range(n_ctx)]
            + [(1, prepare(1, lfbc_ref[0, j * c:(j + 1) * c, :], None), vtc_ref[j], None)
               for j in reversed(range(n_ctx))])

    nw = nw_ref[...]

    def finish(rows, o):
        normed = [_rms(o[:, hs]) * nw for hs in heads]
        o_ref[0, rows, :] = (jnp.concatenate(normed, axis=-1) * g_ref[0, rows, :].astype(F32)).astype(BF16)

    unroll = SCAN_UNROLL
    n_steps = n_lat // unroll

    def step_chunks(j):
        idx = [unroll * j + u for u in range(unroll)] + [n_lat - 1 - (unroll * j + u) for u in range(unroll)]
        start = (lambda i: i * c) if isinstance(j, int) else (lambda i: pl.multiple_of(i * c, c))
        return [(0 if n < unroll else 1, i, pl.ds(start(i), c)) for n, i in enumerate(idx)]

    def log_forget(d, r):
        return (lff_ref if d == 0 else lfb_ref)[0, r, :]

    def step(j, final):
        todo = step_chunks(j)
        cums = [cumulate(d, log_forget(d, r)) for d, _, r in todo]
        ops = [prepare(d, log_forget(d, r), q_ref[0, r, :], b) for (d, _, r), b in zip(todo, cums)]
        outs = advance([(d, p, vt_ref[i], v_ref[0, r, :]) for (d, i, r), p in zip(todo, ops)])
        for (_, _, r), o in zip(todo, outs):
            if final:
                finish(r, oacc_ref[r, :] + o)
            else:
                oacc_ref[r, :] = o

    def first_half(j, carry):
        step(j, False)
        return carry

    def second_half(j, carry):
        step(j, True)
        return carry

    lax.fori_loop(0, n_steps // 2, first_half, 0)
    lax.fori_loop(n_steps // 2, n_steps, second_half, 0)


def _hgrn2(q, v, lff, lfb, g, vc, lffc, lfbc, norm_w):
    b, l, _ = q.shape
    lc = vc.shape[1]
    c = SCAN_CHUNK
    seq = lambda n: pl.BlockSpec((1, n, KW), lambda i: (i, 0, 0))
    return pl.pallas_call(
        _scan_kernel,
        out_shape=jax.ShapeDtypeStruct((b, l, KW), BF16),
        grid=(b,),
        in_specs=[seq(l), seq(l), seq(l), seq(l), seq(l), seq(lc), seq(lc), seq(lc),
                  _const_spec(norm_w.shape)],
        out_specs=seq(l),
        scratch_shapes=[pltpu.VMEM((l, KW), F32), pltpu.VMEM((l // c, KW, c), BF16),
                        pltpu.VMEM((lc // c, KW, c), BF16), pltpu.VMEM((2, HEADS, HEAD_DIM, HEAD_DIM), F32)],
        compiler_params=_params(1),
        name="scan",
    )(q, v, lff, lfb, g, vc, lffc, lfbc, norm_w)


def _filter_kernel(z_ref, w1_ref, b1_ref, f1_ref, w2_ref, b2_ref, f2_ref, w3_ref, win_ref, h_ref, mass_ref):
    hp = dict(precision=HIGHEST, preferred_element_type=F32)
    h = jnp.sin(f1_ref[...] * (jnp.dot(z_ref[...], w1_ref[...], **hp) + b1_ref[...]))
    h = jnp.sin(f2_ref[...] * (jnp.dot(h, w2_ref[...], **hp) + b2_ref[...]))
    (h_hi, h_lo), (w_hi, w_lo) = _split2(h), _split2(w3_ref[...])
    h = _dot(h_hi, w_hi) + _dot(h_hi, w_lo) + _dot(h_lo, w_hi)
    win = win_ref[...]
    n_groups = h.shape[1] // HY_W
    h = jnp.concatenate([h[:, k * HY_W:(k + 1) * HY_W] * win for k in range(n_groups)], axis=-1)
    h_ref[...] = h.astype(BF16)

    @pl.when(pl.program_id(0) == 0)
    def _():
        mass_ref[...] = jnp.zeros(mass_ref.shape, F32)

    mass_ref[...] += jnp.sum(jnp.abs(h), axis=0, keepdims=True)


def _hyena_filters(l, w1, b1, fr1, w2, b2, fr2, w3):
    p = jnp.concatenate([jnp.arange(r, l, TIME_SPLIT) for r in range(TIME_SPLIT)]).astype(F32)
    t = p / (l - 1)
    w = 2.0 * math.pi * p / l
    f = jnp.linspace(1e-4, HYENA_BANDS - 1, HYENA_BANDS, dtype=F32)
    ang = w[:, None] * f[None, :]
    z = jnp.concatenate([t[:, None], jnp.cos(ang), -jnp.sin(ang)], axis=-1)
    max_decay = math.log(HYENA_TARGET) / HYENA_FAST_DECAY
    min_decay = math.log(HYENA_TARGET) / HYENA_SLOW_DECAY
    deltas = jnp.abs(jnp.linspace(min_decay, max_decay, HY_W, dtype=F32))
    window = jnp.exp(-t[:, None] * deltas[None, :]) + HYENA_SHIFT

    pad_c = lambda a, n: jnp.pad(a, ((0, 0), (0, n - a.shape[1])))
    pad_r = lambda a, n: jnp.pad(a, ((0, n - a.shape[0]), (0, 0)))
    z = pad_c(z, FEAT_PAD)
    w1p = pad_c(pad_r(w1, FEAT_PAD), FEAT_PAD)
    w2p = pad_c(pad_r(w2, FEAT_PAD), FEAT_PAD)
    w3p = pad_r(w3, FEAT_PAD)
    vec = lambda a: pad_c(a[None, :], FEAT_PAD)
    n_out = w3.shape[1]
    tl = FILT_TILE
    return pl.pallas_call(
        _filter_kernel,
        out_shape=[jax.ShapeDtypeStruct((l, n_out), BF16), jax.ShapeDtypeStruct((1, n_out), F32)],
        grid=(l // tl,),
        in_specs=[
            pl.BlockSpec((tl, FEAT_PAD), lambda i: (i, 0)),
            _const_spec(w1p.shape), _const_spec((1, FEAT_PAD)), _const_spec((1, FEAT_PAD)),
            _const_spec(w2p.shape), _const_spec((1, FEAT_PAD)), _const_spec((1, FEAT_PAD)),
            _const_spec(w3p.shape),
            pl.BlockSpec((tl, HY_W), lambda i: (i, 0)),
        ],
        out_specs=[pl.BlockSpec((tl, n_out), lambda i: (i, 0)), pl.BlockSpec((1, n_out), lambda i: (0, 0))],
        compiler_params=_params(1),
        name="filt",
    )(z, w1p, vec(b1), vec(fr1), w2p, vec(b2), vec(fr2), w3p, window)


def _dft_kernel(ar_ref, ai_ref, br_ref, bi_ref, cr_ref, ci_ref, dr_ref, di_ref, fwd_ref, inv_ref):
    tf = FREQ_TILE
    br, bi = br_ref[...], bi_ref[...]
    for t1 in range(fwd_ref.shape[1] // LANES):
        ar, ai = ar_ref[:, t1:t1 + 1], ai_ref[:, t1:t1 + 1]
        cols = slice(t1 * LANES, (t1 + 1) * LANES)
        fwd_ref[0:tf, cols] = (ar * br - ai * bi).astype(BF16)
        fwd_ref[tf:2 * tf, cols] = (ar * bi + ai * br).astype(BF16)
    dr, di = dr_ref[...], di_ref[...]
    for f1 in range(tf // LANES):
        cr, ci = cr_ref[0, :, f1:f1 + 1], ci_ref[0, :, f1:f1 + 1]
        inv_ref[0, :, f1 * LANES:(f1 + 1) * LANES] = (cr * dr - ci * di).astype(BF16)
        inv_ref[0, :, tf + f1 * LANES:tf + (f1 + 1) * LANES] = (cr * di + ci * dr).astype(BF16)


def _odd_dft_matrices(l):
    n_ang = 4 * l
    theta = 2.0 * math.pi / n_ang
    tf = FREQ_TILE
    nf = l // tf
    n_hi = l // LANES

    def cis(idx):
        a = (idx % n_ang).astype(F32) * theta
        return jnp.cos(a), jnp.sin(a)

    idx = jnp.arange(l, dtype=jnp.int32)
    odd = 2 * idx + 1
    lane = jnp.arange(LANES, dtype=jnp.int32)
    ar, ai = cis(odd[:, None] * (LANES * jnp.arange(n_hi, dtype=jnp.int32))[None, :])
    br, bi = cis(odd[:, None] * lane[None, :])
    g = jnp.arange(n_hi, dtype=jnp.int32).reshape(nf, 1, tf // LANES)
    cr, ci = cis(idx[None, :, None] * (2 * LANES * g))
    dr, di = cis(idx[:, None] * (2 * lane + 1)[None, :])
    row_tab = lambda w: pl.BlockSpec((tf, w), lambda i: (i, 0))
    return pl.pallas_call(
        _dft_kernel,
        out_shape=[jax.ShapeDtypeStruct((2 * l, l), BF16), jax.ShapeDtypeStruct((nf, l, 2 * tf), BF16)],
        grid=(nf,),
        in_specs=[row_tab(n_hi), row_tab(n_hi), row_tab(LANES), row_tab(LANES),
                  pl.BlockSpec((1, l, tf // LANES), lambda i: (i, 0, 0)),
                  pl.BlockSpec((1, l, tf // LANES), lambda i: (i, 0, 0)),
                  _const_spec((l, LANES)), _const_spec((l, LANES))],
        out_specs=[pl.BlockSpec((2 * tf, l), lambda i: (i, 0)), pl.BlockSpec((1, l, 2 * tf), lambda i: (i, 0, 0))],
        compiler_params=_params(1),
        name="dft",
    )(ar, ai, br, bi, cr, ci, dr, di)


def _butterfly(ac, as_, bc, bs, cph, sph):
    tc = cph * bc - sph * bs
    ts = cph * bs + sph * bc
    return ac + tc, as_ + ts, ac - tc, ts - as_


def _inv_butterfly(yc, ys, yhc, yhs, cph, sph):
    dc, ds = yc - yhc, ys + yhs
    return yc + yhc, ys - yhs, dc * cph + ds * sph, ds * cph - dc * sph


def _split4_forward(fc, fs, xs, tw):
    c1, s1, c1g, s1g, c2, s2 = (tw[:, k:k + 1] for k in range(6))
    p = [(_dot(fc, x).astype(tw.dtype), _dot(fs, x).astype(tw.dtype)) for x in xs]
    ev = _butterfly(*p[0], *p[2], c2, s2)
    od = _butterfly(*p[1], *p[3], c2, s2)
    return _butterfly(*ev[:2], *od[:2], c1, s1) + _butterfly(*ev[2:], *od[2:], c1g, s1g)


def _split4_inverse(y, tw):
    c1, s1, c1g, s1g, c2, s2 = (tw[:, k:k + 1] for k in range(6))
    at_f = _inv_butterfly(*y[0:4], c1, s1)
    at_g = _inv_butterfly(*y[4:8], c1g, s1g)
    r0c, r0s, r2c, r2s = _inv_butterfly(*at_f[:2], *at_g[:2], c2, s2)
    r1c, r1s, r3c, r3s = _inv_butterfly(*at_f[2:], *at_g[2:], c2, s2)
    return [(r0c, r0s), (r1c, r1s), (r2c, r2s), (r3c, r3s)]


def _kdft_kernel(fwd_ref, t0_ref, t1_ref, t2_ref, t3_ref, mass_ref, tw_ref, o_ref):
    tf = FREQ_TILE
    taps = (t0_ref, t1_ref, t2_ref, t3_ref)
    l = TIME_SPLIT * t0_ref.shape[0]
    fc, fs = fwd_ref[0:tf, :], fwd_ref[tf:2 * tf, :]
    tw = tw_ref[...]
    f = pl.program_id(0) * tf + lax.broadcasted_iota(jnp.int32, (tf, 1), 0)
    sgn_f = jnp.where(f % 2 == 0, 1.0, -1.0)
    signs = (sgn_f, -sgn_f, -sgn_f, sgn_f)
    for order in range(2):
        fw = slice((2 * order) * HY_W, (2 * order + 1) * HY_W)
        bw = slice((2 * order + 1) * HY_W, (2 * order + 2) * HY_W)
        a = _split4_forward(fc, fs, [t[:, fw] for t in taps], tw)
        h = _split4_forward(fc, fs, [t[:, bw] for t in taps], tw)
        scale = (1.0 / l) / (mass_ref[:, fw] + mass_ref[:, bw] + HYENA_L1_EPS)
        cols = slice(order * HY_W, (order + 1) * HY_W)
        for k in range(4):
            ac, as_, hc, hs = a[2 * k], a[2 * k + 1], h[2 * k], h[2 * k + 1]
            ec, es = tw[:, 6 + 2 * k:7 + 2 * k], tw[:, 7 + 2 * k:8 + 2 * k]
            o_ref[0, 2 * k, :, cols] = ((ac + signs[k] * (es * hc - ec * hs)) * scale).astype(BF16)
            o_ref[0, 2 * k + 1, :, cols] = ((-as_ + signs[k] * (ec * hc + es * hs)) * scale).astype(BF16)


def _twiddles(l):
    n_ang = 4 * l
    theta = 2.0 * math.pi / n_ang
    f = jnp.arange(l // 4, dtype=jnp.int32)
    g = l // 2 - 1 - f
    odd = lambda x: 2 * x + 1
    angles = [odd(f), odd(g), 2 * odd(f)] + [odd(x) * (l - 1) for x in (f, l - 1 - f, g, l - 1 - g)]
    cols = []
    for a in angles:
        r = (a % n_ang).astype(F32) * theta
        cols += [jnp.cos(r), jnp.sin(r)]
    cols += [jnp.zeros_like(cols[0])] * 2
    return jnp.stack(cols, axis=-1)


def _filter_spectrum(fwd, taps, mass, tw):
    l = taps.shape[0]
    m = l // TIME_SPLIT
    tf = FREQ_TILE
    tap_block = lambda r: pl.BlockSpec((m, taps.shape[1]), lambda i: (r, 0), pipeline_mode=pl.Buffered(1))
    return pl.pallas_call(
        _kdft_kernel,
        out_shape=jax.ShapeDtypeStruct((m // tf, 8, tf, 2 * HY_W), BF16),
        grid=(m // tf,),
        in_specs=[pl.BlockSpec((2 * tf, m), lambda i: (i, 0))] + [tap_block(r) for r in range(TIME_SPLIT)]
        + [_const_spec(mass.shape), pl.BlockSpec((tf, tw.shape[1]), lambda i: (i, 0))],
        out_specs=pl.BlockSpec((1, 8, tf, 2 * HY_W), lambda i: (i, 0, 0, 0)),
        compiler_params=_params(1),
        name="kdft",
    )(fwd, taps, taps, taps, taps, mass, tw)


HYENA_SUB = 128


def _short_conv_split(x_ref, cw, cb):
    x = [x_ref[0, r].astype(F32) for r in range(TIME_SPLIT)]
    m = x[0].shape[0]
    t = lax.broadcasted_iota(jnp.int32, (m, 1), 0)
    before = jnp.where(t == 0, 0.0, pltpu.roll(x[-1], 1, 0))
    after = jnp.where(t == m - 1, 0.0, pltpu.roll(x[0], m - 1, 0))
    prev = [before] + x[:-1]
    nxt = x[1:] + [after]
    return [cb + prev[r] * cw[0:1, :] + x[r] * cw[1:2, :] + nxt[r] * cw[2:3, :] for r in range(TIME_SPLIT)]


def _hyena_kernel(hv_ref, hx_ref, cw_ref, cb_ref, hb_ref, tw_ref, fwd_ref, inv_ref, kt_ref,
                  o_ref, u_ref, acc_ref, y_ref, nat_ref):
    order = pl.program_id(1)
    i = pl.program_id(2)
    tf = FREQ_TILE
    m, wide = u_ref.shape
    w = wide // TIME_SPLIT
    last = m // tf - 1
    lanes = [slice(r * w, (r + 1) * w) for r in range(TIME_SPLIT)]

    def restart(parts, bias):
        for r, part in enumerate(parts):
            u_ref[:, lanes[r]] = part.astype(BF16)
            acc_ref[:, lanes[r]] = bias * part

    @pl.when((order == 0) & (i == 0))
    def _():
        restart(_short_conv_split(hv_ref, cw_ref[0], cb_ref[0]), hb_ref[0:1, :])

    xs = [u_ref[:, ln] for ln in lanes]
    for sb in range(tf // HYENA_SUB):
        rc = slice(sb * HYENA_SUB, (sb + 1) * HYENA_SUB)
        rs = slice(tf + sb * HYENA_SUB, tf + (sb + 1) * HYENA_SUB)
        tw = tw_ref[rc, :].astype(BF16)
        x = _split4_forward(fwd_ref[rc, :], fwd_ref[rs, :], xs, tw)
        y = []
        for k in range(4):
            kr, ki = kt_ref[0, 2 * k, rc, :], kt_ref[0, 2 * k + 1, rc, :]
            y += [kr * x[2 * k] + ki * x[2 * k + 1], kr * x[2 * k + 1] - ki * x[2 * k]]
        for r, (yc, ys) in enumerate(_split4_inverse(y, tw)):
            y_ref[rc, lanes[r]] = yc
            y_ref[rs, lanes[r]] = ys
    acc_ref[...] += _dot(inv_ref[0], y_ref[...])

    @pl.when((order == 0) & (i == last))
    def _():
        gates = _short_conv_split(hx_ref, cw_ref[1], cb_ref[1])
        restart([g * acc_ref[:, ln] for g, ln in zip(gates, lanes)], hb_ref[1:2, :])

    @pl.when((order == 1) & (i == last))
    def _():
        gates = _short_conv_split(hx_ref, cw_ref[2], cb_ref[2])
        z = [g * acc_ref[:, ln] for g, ln in zip(gates, lanes)]
        for k in range(w // LANES):
            cols = slice(k * LANES, (k + 1) * LANES)
            for r in range(TIME_SPLIT):
                nat_ref[k, pl.ds(r, m, stride=TIME_SPLIT), :] = z[r][:, cols]
            o_ref[0, :, cols] = nat_ref[k].astype(BF16)


def _hyena(hy, conv_w, conv_b, hy_bias, tw, fwd, inv, ktab):
    b, _, m, _ = hy.shape
    tf = FREQ_TILE
    nf = m // tf
    cw = conv_w.reshape(conv_w.shape[0], 3, HY_W).transpose(1, 0, 2)
    cb = conv_b.reshape(3, 1, HY_W)
    return pl.pallas_call(
        _hyena_kernel,
        out_shape=jax.ShapeDtypeStruct((b, TIME_SPLIT * m, HY_W), BF16),
        grid=(b, 2, nf),
        in_specs=[
            pl.BlockSpec((1, TIME_SPLIT, m, HY_W), lambda bi, o, i: (bi, 0, 0, 0)),
            pl.BlockSpec((1, TIME_SPLIT, m, HY_W), lambda bi, o, i: (bi, 0, 0, 1 + o)),
            _const_spec(cw.shape), _const_spec(cb.shape), _const_spec(hy_bias.shape),
            pl.BlockSpec((tf, tw.shape[1]), lambda bi, o, i: (i, 0)),
            pl.BlockSpec((2 * tf, m), lambda bi, o, i: (i, 0)),
            pl.BlockSpec((1, m, 2 * tf), lambda bi, o, i: (i, 0, 0)),
            pl.BlockSpec((1, 8, tf, HY_W), lambda bi, o, i: (i, 0, 0, o)),
        ],
        out_specs=pl.BlockSpec((1, TIME_SPLIT * m, HY_W), lambda bi, o, i: (bi, 0, 0)),
        scratch_shapes=[pltpu.VMEM((m, TIME_SPLIT * HY_W), BF16), pltpu.VMEM((m, TIME_SPLIT * HY_W), F32),
                        pltpu.VMEM((2 * tf, TIME_SPLIT * HY_W), BF16),
                        pltpu.VMEM((HY_W // LANES, TIME_SPLIT * m, LANES), F32)],
        compiler_params=_params(3),
        name="hyena",
    )(hy, hy, cw, cb, hy_bias, tw, fwd, inv, ktab)


def _grid_pos_embed(n_tokens):
    rows = n_tokens // GRID_W
    quarter = D_MODEL // 4
    omega = 1.0 / (10000.0 ** (jnp.arange(quarter, dtype=F32) / quarter))
    ar = jnp.arange(rows, dtype=F32)[:, None] * omega
    ac = jnp.arange(GRID_W, dtype=F32)[:, None] * omega
    er = jnp.concatenate([jnp.sin(ar), jnp.cos(ar)], axis=-1)
    ec = jnp.concatenate([jnp.sin(ac), jnp.cos(ac)], axis=-1)
    emb = jnp.concatenate([jnp.broadcast_to(er[:, None, :], (rows, GRID_W, D_MODEL // 2)),
                           jnp.broadcast_to(ec[None, :, :], (rows, GRID_W, D_MODEL // 2))], axis=-1)
    return emb.reshape(rows * GRID_W, D_MODEL)


def kernel(x, c, ctx, c_ctx, mod_w, mod_b, ffn_w_gate, ffn_w_up, ffn_w_down, w_in, hgrn_lb_logits,
           hgrn_norm_w, hyena_conv_w, hyena_conv_b, hyena_w1, hyena_b1, hyena_freq1, hyena_w2, hyena_b2,
           hyena_freq2, hyena_w3, hyena_bias, w_proj_a, w_proj_b, w_out, final_norm_w):
    assert mod_w.shape[0] == 1, "single-layer configuration"
    batch, n_lat, d = x.shape

    c_all = jnp.concatenate([c, c_ctx[None, :]], axis=0)
    c_all = jnp.pad(c_all, ((0, -c_all.shape[0] % 8), (0, 0)))
    m3 = _modulation(c_all, mod_w[0], mod_b[0][None, :]).reshape(c_all.shape[0], N_MOD, d)

    lb = jnp.cumsum(jax.nn.softmax(hgrn_lb_logits.astype(F32), axis=0), axis=0)[0]
    wg, wu, wd = _to_bf16(ffn_w_gate[0]), _to_bf16(ffn_w_up[0]), _to_bf16(ffn_w_down[0])
    w_in_b = _to_bf16(w_in[0])

    h1 = _half_ffn(x, m3, wg, wu, wd, 0, mod_base=0, pos=_grid_pos_embed(n_lat))
    n_ctx = ctx.shape[1]
    slab = math.gcd(batch * n_ctx, FFN_TILE)
    hc1 = _half_ffn(ctx.reshape(-1, slab, d), m3, wg, wu, wd, 0, mod_base=0, mod_row=batch)

    vc, lffc, lfbc = (a.reshape(batch, n_ctx, a.shape[-1])
                      for a in _input_proj(hc1, m3, w_in_b, lb, mod_row=batch))
    v, lff, lfb, q, g, hy, sg = _input_proj(h1, m3, w_in_b, lb)
    o_a = _hgrn2(q, v, lff, lfb, g, vc, lffc, lfbc, hgrn_norm_w[0][None, :])

    taps, mass = _hyena_filters(n_lat, hyena_w1[0], hyena_b1[0], hyena_freq1[0], hyena_w2[0], hyena_b2[0],
                                hyena_freq2[0], hyena_w3[0])
    tw = _twiddles(n_lat)
    fwd, inv = _odd_dft_matrices(n_lat // TIME_SPLIT)
    ktab = _filter_spectrum(fwd, taps, mass, tw)
    o_b = _hyena(hy, hyena_conv_w[0], hyena_conv_b[0], hyena_bias[0], tw, fwd, inv, ktab)

    mixers = (o_a, o_b, sg, _to_bf16(w_proj_a[0]), _to_bf16(w_proj_b[0]), _to_bf16(w_out[0]))
    return _half_ffn(h1, m3, wg, wu, wd, 1, mod_base=6, mixers=mixers, final_norm_w=final_norm_w[None, :])
```
